```python
import math
import jax, jax.numpy as jnp
from jax import lax
import numpy as np

D_MODEL = 2048
BATCH = 8
SEQ = 2048
DEPTH = 1
DEC_BATCH = 32
DEC_SEQ = 8
PAST_LEN = 16384
PAGE_SIZE = 128

A_WIDTH = D_MODEL // 4
A_GROUPS = 4
A_DIM = A_WIDTH // A_GROUPS
CHUNK = 128
B_WIDTH = D_MODEL // 2
B_HEAD_DIM = 64
B_HEADS = B_WIDTH // B_HEAD_DIM
M_WIDTH = D_MODEL // 4
M_HEADS = 4
M_HEAD_DIM = M_WIDTH // M_HEADS
N_MEM = 256
DILATIONS = ((128, 1), (512, 4), (2048, 16))
WINDOW_MAX = 2048
N_BUCKETS = 32
MAX_DISTANCE = 2048
N_GROUPS = 4
EXPERTS_PER_GROUP = 8
N_EXPERTS = N_GROUPS * EXPERTS_PER_GROUP
TOP_K = 2
EXPERT_FF = D_MODEL // 8
IN_COLS = 2 * A_WIDTH + 3 * B_WIDTH + M_WIDTH
EPS = 1e-6

kernel_name = "hybrid_gmlp_dilated_memory_hmoe_step"


def rms_norm(x, g):
    xf = x.astype(jnp.float32)
    y = xf * lax.rsqrt(jnp.mean(xf * xf, axis=-1, keepdims=True) + EPS)
    return (y * g.astype(jnp.float32)).astype(x.dtype)


def t5_bucket(dist):
    max_exact = N_BUCKETS // 2
    d = jnp.maximum(dist, 0)
    df = jnp.maximum(d, 1).astype(jnp.float32)
    large = max_exact + (jnp.log(df / max_exact) / math.log(MAX_DISTANCE / max_exact)
                         * (N_BUCKETS - max_exact)).astype(jnp.int32)
    large = jnp.minimum(large, N_BUCKETS - 1)
    return jnp.where(d < max_exact, d, large)


def mixer_inputs(x, g_mix, w_in, g_av, g_qb, g_kb, g_qm):
    bt, s, _ = x.shape
    z = jnp.einsum("bsd,dc->bsc", rms_norm(x, g_mix), w_in)
    o_b = 2 * A_WIDTH
    o_m = o_b + 3 * B_WIDTH
    uv = jax.nn.gelu(z[..., :o_b], approximate=False)
    u = uv[..., :A_WIDTH].reshape(bt, s, A_GROUPS, A_DIM)
    va = rms_norm(uv[..., A_WIDTH:].reshape(bt, s, A_GROUPS, A_DIM), g_av)
    qkv = z[..., o_b:o_m].reshape(bt, s, 3, B_HEADS, B_HEAD_DIM)
    qb = rms_norm(qkv[:, :, 0], g_qb)
    kb = rms_norm(qkv[:, :, 1], g_kb)
    vb = qkv[:, :, 2]
    qm = rms_norm(z[..., o_m:].reshape(bt, s, M_HEADS, M_HEAD_DIM), g_qm)
    return u, va, qb, kb, vb, qm


def spatial_gate(u, v, w_s, b_s, c):
    bt, s, g, ch = v.shape
    vc = v.reshape(bt, s // c, c, g, ch)
    w = jnp.where(jnp.tril(jnp.ones((c, c), dtype=bool)), w_s[:, :c, :c], 0)
    mixed = jnp.einsum("gts,bnsgc->bntgc", w, vc) + b_s[:, :c].T[None, None, :, :, None]
    return u * mixed.reshape(bt, s, g, ch)


def combine_by_denominator(outs, lses):
    wts = jax.nn.softmax(jnp.stack(lses, axis=0), axis=0)
    return jnp.sum(wts[..., None] * jnp.stack(outs, axis=0), axis=0)


def dilated_attend_prompt(q, k, v, rel_bias):
    dtype = q.dtype
    q, k, v = q.astype(jnp.float32), k.astype(jnp.float32), v.astype(jnp.float32)
    bsz, s, h, dh = q.shape
    scale = B_HEAD_DIM ** -0.5
    outs, lses = [], []
    for window, dil in DILATIONS:
        blk = window // dil
        unit = blk * dil
        sp = -(-s // unit) * unit
        nb = sp // unit

        def to_sub(t):
            t = jnp.pad(t, ((0, 0), (0, sp - s), (0, 0), (0, 0)))
            t = t.reshape(bsz, sp // dil, dil, h, dh).transpose(0, 2, 1, 3, 4)
            return t.reshape(bsz, dil, nb, blk, h, dh)

        def with_prev(t):
            prev = jnp.pad(t, ((0, 0), (0, 0), (1, 0), (0, 0), (0, 0), (0, 0)))[:, :, :nb]
            return jnp.concatenate([prev, t], axis=3)

        qs = to_sub(q)
        kk, vv = with_prev(to_sub(k)), with_prev(to_sub(v))
        qi = jnp.arange(blk)[:, None] + blk
        kj = jnp.arange(2 * blk)[None, :]
        delta = qi - kj
        bias = jnp.transpose(rel_bias[t5_bucket(delta * dil)].astype(jnp.float32), (2, 0, 1))
        mk = jnp.arange(nb)[:, None, None] * blk + kj[None] - blk
        valid = (delta >= 0)[None] & (delta <= blk)[None] & (mk >= 0)
        sc = jnp.einsum("brnqhe,brnkhe->brnhqk", qs, kk) * scale + bias
        sc = jnp.where(valid[None, None, :, None], sc, -jnp.inf)
        m = jnp.max(sc, axis=-1, keepdims=True)
        p = jnp.exp(sc - m)
        l = jnp.sum(p, axis=-1, keepdims=True)
        o = jnp.einsum("brnhqk,brnkhe->brnqhe", p / l, vv)
        lse = (m + jnp.log(l))[..., 0]
        o = o.reshape(bsz, dil, sp // dil, h, dh).transpose(0, 2, 1, 3, 4).reshape(bsz, sp, h, dh)[:, :s]
        lse = lse.transpose(0, 1, 2, 4, 3).reshape(bsz, dil, sp // dil, h).transpose(0, 2, 1, 3)
        outs.append(o)
        lses.append(lse.reshape(bsz, sp, h)[:, :s])
    return combine_by_denominator(outs, lses).astype(dtype)


def dilated_attend_sample(q, k_all, v_all, rel_bias):
    dtype = q.dtype
    q, k_all, v_all = q.astype(jnp.float32), k_all.astype(jnp.float32), v_all.astype(jnp.float32)
    ds = q.shape[1]
    base = k_all.shape[1] - ds
    scale = B_HEAD_DIM ** -0.5
    outs, lses = [], []
    for window, dil in DILATIONS:
        n = jnp.arange(window // dil + 1)
        idx = base + jnp.arange(ds)[:, None] - n[None, :] * dil
        valid = idx >= 0
        idc = jnp.maximum(idx, 0)
        kg = k_all[:, idc]
        vg = v_all[:, idc]
        bias = rel_bias[t5_bucket(n * dil)].astype(jnp.float32).T
        sc = jnp.einsum("bqhe,bqkhe->bhqk", q, kg) * scale + bias[None, :, None, :]
        sc = jnp.where(valid[None, None], sc, -jnp.inf)
        m = jnp.max(sc, axis=-1, keepdims=True)
        p = jnp.exp(sc - m)
        l = jnp.sum(p, axis=-1, keepdims=True)
        outs.append(jnp.einsum("bhqk,bqkhe->bqhe", p / l, vg))
        lses.append(jnp.transpose((m + jnp.log(l))[..., 0], (0, 2, 1)))
    return combine_by_denominator(outs, lses).astype(dtype)


def memory_attend(q, mk, mv):
    dtype = q.dtype
    sc = jnp.einsum("bshe,bmhe->bhsm", q.astype(jnp.float32), mk.astype(jnp.float32)) * (M_HEAD_DIM ** -0.5)
    p = jax.nn.softmax(sc, axis=-1)
    return jnp.einsum("bhsm,bmhe->bshe", p, mv.astype(jnp.float32)).astype(dtype)


def memory_kv(mem, g_mem, w_mem_kv, g_km):
    bt = mem.shape[0]
    kv = jnp.einsum("bmd,dc->bmc", rms_norm(mem, g_mem), w_mem_kv)
    mk = rms_norm(kv[..., :M_WIDTH].reshape(bt, N_MEM, M_HEADS, M_HEAD_DIM), g_km)
    mv = kv[..., M_WIDTH:].reshape(bt, N_MEM, M_HEADS, M_HEAD_DIM)
    return mk, mv


def hier_moe(n, w_rg, w_re, w_gate, w_up, w_down):
    bt, s, d = n.shape
    t = n.reshape(bt * s, d)
    g_logits = jnp.einsum("nd,dg->ng", t, w_rg).astype(jnp.float32)
    grp = jnp.argmax(g_logits, axis=-1)
    e_logits = jnp.einsum("nd,gde->nge", t, w_re).astype(jnp.float32)
    e_sel = jnp.take_along_axis(e_logits, grp[:, None, None], axis=1)[:, 0]
    top_v, top_i = lax.top_k(e_sel, TOP_K)
    top_w = jax.nn.softmax(top_v, axis=-1)
    expert_id = grp[:, None] * EXPERTS_PER_GROUP + top_i
    gates = jnp.sum(jax.nn.one_hot(expert_id, N_EXPERTS, dtype=jnp.float32) * top_w[..., None], axis=1)
    hg = jnp.einsum("nd,edf->nef", t, w_gate)
    hu = jnp.einsum("nd,edf->nef", t, w_up)
    act = jax.nn.silu(hg) * hu * gates[..., None].astype(t.dtype)
    return jnp.einsum("nef,efd->nd", act, w_down).reshape(bt, s, d)


def mixer_output_and_ffn(x, ya, yb, ym, w_out, g_ffn, w_rg, w_re, w_gate, w_up, w_down):
    bt, s, _ = x.shape
    mixed = jnp.concatenate([ya.reshape(bt, s, A_WIDTH), yb.reshape(bt, s, B_WIDTH),
                             ym.reshape(bt, s, M_WIDTH)], axis=-1)
    h = x + jnp.einsum("bsc,cd->bsd", mixed, w_out)
    return h + hier_moe(rms_norm(h, g_ffn), w_rg, w_re, w_gate, w_up, w_down)


def setup_inputs(seed: int = 0) -> dict:
    key = jax.random.key(seed)
    ks = jax.random.split(key, 32)
    f32 = jnp.float32

    def nrm(k, shape, scale=1.0):
        return jax.random.normal(k, shape, f32) * scale

    def gain(k, shape):
        return 1.0 + 0.01 * jax.random.normal(k, shape, f32)

    wb = min(WINDOW_MAX, PAST_LEN)
    dsc = D_MODEL ** -0.5
    return {
        "x_prompt": nrm(ks[0], (BATCH, SEQ, D_MODEL)),
        "x_sample": nrm(ks[1], (DEC_BATCH, DEC_SEQ, D_MODEL)),
        "mem_prompt": nrm(ks[2], (BATCH, N_MEM, D_MODEL)),
        "cache_win_k": nrm(ks[3], (DEPTH, DEC_BATCH, wb, B_HEADS, B_HEAD_DIM)),
        "cache_win_v": nrm(ks[4], (DEPTH, DEC_BATCH, wb, B_HEADS, B_HEAD_DIM)),
        "cache_mem_k": nrm(ks[5], (DEPTH, DEC_BATCH, N_MEM, M_HEADS, M_HEAD_DIM)),
        "cache_mem_v": nrm(ks[6], (DEPTH, DEC_BATCH, N_MEM, M_HEADS, M_HEAD_DIM)),
        "rel_bias": nrm(ks[7], (N_BUCKETS, B_HEADS), 0.2),
        "g_mix": gain(ks[8], (DEPTH, D_MODEL)),
        "w_in": nrm(ks[9], (DEPTH, D_MODEL, IN_COLS), dsc),
        "g_av": gain(ks[10], (DEPTH, A_GROUPS, A_DIM)),
        "w_s": nrm(ks[11], (DEPTH, A_GROUPS, CHUNK, CHUNK), CHUNK ** -0.5),
        "b_s": nrm(ks[12], (DEPTH, A_GROUPS, CHUNK), 0.1),
        "g_qb": gain(ks[13], (DEPTH, B_HEAD_DIM)),
        "g_kb": gain(ks[14], (DEPTH, B_HEAD_DIM)),
        "g_qm": gain(ks[15], (DEPTH, M_HEAD_DIM)),
        "g_km": gain(ks[16], (DEPTH, M_HEAD_DIM)),
        "g_mem": gain(ks[17], (DEPTH, D_MODEL)),
        "w_mem_kv": nrm(ks[18], (DEPTH, D_MODEL, 2 * M_WIDTH), dsc),
        "w_out": nrm(ks[19], (DEPTH, D_MODEL, D_MODEL), dsc),
        "g_ffn": gain(ks[20], (DEPTH, D_MODEL)),
        "w_router_group": nrm(ks[21], (DEPTH, D_MODEL, N_GROUPS), dsc),
        "w_router_expert": nrm(ks[22], (DEPTH, N_GROUPS, D_MODEL, EXPERTS_PER_GROUP), dsc),
        "w_gate": nrm(ks[23], (DEPTH, N_EXPERTS, D_MODEL, EXPERT_FF), dsc),
        "w_up": nrm(ks[24], (DEPTH, N_EXPERTS, D_MODEL, EXPERT_FF), dsc),
        "w_down": nrm(ks[25], (DEPTH, N_EXPERTS, EXPERT_FF, D_MODEL), EXPERT_FF ** -0.5),
    }


def reference(x_prompt, x_sample, mem_prompt, cache_win_k, cache_win_v, cache_mem_k, cache_mem_v,
              rel_bias, g_mix, w_in, g_av, w_s, b_s, g_qb, g_kb, g_qm, g_km, g_mem, w_mem_kv, w_out,
              g_ffn, w_router_group, w_router_expert, w_gate, w_up, w_down):
    xp, xs = x_prompt, x_sample
    s = xp.shape[1]
    ds = xs.shape[1]
    wp = min(WINDOW_MAX, s)
    pk, pv, pc, pmk, pmv, sk, sv, sc_ = [], [], [], [], [], [], [], []
    for l in range(DEPTH):
        u, va, qb, kb, vb, qm = mixer_inputs(xp, g_mix[l], w_in[l], g_av[l], g_qb[l], g_kb[l], g_qm[l])
        ya = spatial_gate(u, va, w_s[l], b_s[l], CHUNK)
        yb = dilated_attend_prompt(qb, kb, vb, rel_bias)
        mk, mv = memory_kv(mem_prompt, g_mem[l], w_mem_kv[l], g_km[l])
        ym = memory_attend(qm, mk, mv)
        xp = mixer_output_and_ffn(xp, ya, yb, ym, w_out[l], g_ffn[l], w_router_group[l],
                                  w_router_expert[l], w_gate[l], w_up[l], w_down[l])
        pk.append(kb[:, s - wp:])
        pv.append(vb[:, s - wp:])
        pc.append(va[:, s - CHUNK:])
        pmk.append(mk)
        pmv.append(mv)
        u, va, qb, kb, vb, qm = mixer_inputs(xs, g_mix[l], w_in[l], g_av[l], g_qb[l], g_kb[l], g_qm[l])
        ya = spatial_gate(u, va, w_s[l], b_s[l], ds)
        k_all = jnp.concatenate([cache_win_k[l].astype(kb.dtype), kb], axis=1)
        v_all = jnp.concatenate([cache_win_v[l].astype(vb.dtype), vb], axis=1)
        yb = dilated_attend_sample(qb, k_all, v_all, rel_bias)
        ym = memory_attend(qm, cache_mem_k[l], cache_mem_v[l])
        xs = mixer_output_and_ffn(xs, ya, yb, ym, w_out[l], g_ffn[l], w_router_group[l],
                                  w_router_expert[l], w_gate[l], w_up[l], w_down[l])
        sk.append(kb)
        sv.append(vb)
        sc_.append(va)
    win_k_prompt = jnp.stack(pk, axis=0)
    win_v_prompt = jnp.stack(pv, axis=0)
    chunk_v_prompt = jnp.stack(pc, axis=0)
    mem_k_prompt = jnp.stack(pmk, axis=0)
    mem_v_prompt = jnp.stack(pmv, axis=0)
    win_k_sample = jnp.stack(sk, axis=0)
    win_v_sample = jnp.stack(sv, axis=0)
    chunk_v_sample = jnp.stack(sc_, axis=0)
    return (xp, xs, win_k_prompt, win_v_prompt, chunk_v_prompt, mem_k_prompt, mem_v_prompt,
            win_k_sample, win_v_sample, chunk_v_sample)
```

```python
import functools
import math

import numpy as np
import jax
import jax.numpy as jnp
from jax import lax
from jax.experimental import pallas as pl
from jax.experimental.pallas import tpu as pltpu

F32 = jnp.float32
BF16 = jnp.bfloat16

D_MODEL = 2048
A_GROUPS = 4
A_DIM = 128
A_WIDTH = A_GROUPS * A_DIM
CHUNK = 128
B_HEADS = 16
B_HEAD_DIM = 64
B_WIDTH = B_HEADS * B_HEAD_DIM
M_HEADS = 4
M_HEAD_DIM = 128
M_WIDTH = M_HEADS * M_HEAD_DIM
N_MEM = 256
DILATIONS = ((128, 1), (512, 4), (2048, 16))
N_DIL = len(DILATIONS)
KEYS_BACK = 128
N_BUCKETS = 32
MAX_DISTANCE = 2048
N_GROUPS = 4
EXPERTS_PER_GROUP = 8
N_EXPERTS = N_GROUPS * EXPERTS_PER_GROUP
EPS = 1e-6

LANES = 128
V7X_VMEM_BYTES = 64 * 1024 * 1024
VMEM_CAP = V7X_VMEM_BYTES - 6 * 1024 * 1024
INV_SQRT2 = 0.7071067811865476
B_SCALE = B_HEAD_DIM ** -0.5
M_SCALE = M_HEAD_DIM ** -0.5
NEG_INF = float("-inf")

_NT = (((1,), (1,)), ((), ()))
_TN = (((0,), (0,)), ((), ()))


def _log2(n):
    assert n > 0 and n & (n - 1) == 0, n
    return n.bit_length() - 1


def _mm(a, b, dims=None):
    a, b = a.astype(BF16), b.astype(BF16)
    if dims is None:
        return jnp.dot(a, b, preferred_element_type=F32)
    return lax.dot_general(a, b, dims, preferred_element_type=F32)


def _vmem_limit(block_bytes, scratch_bytes=0):
    est = 2 * block_bytes + scratch_bytes + 16 * 1024 * 1024
    return int(min(max(est, 32 * 1024 * 1024), VMEM_CAP))


def _nbytes(shape, dtype):
    return int(np.prod(shape)) * jnp.dtype(dtype).itemsize


def _call(body, grid, in_specs, out_specs, out_shape, operands, name, semantics=None, scratch=()):
    outs = out_shape if isinstance(out_shape, (list, tuple)) else [out_shape]
    ospecs = out_specs if isinstance(out_specs, (list, tuple)) else [out_specs]
    blk = sum(_nbytes(s.block_shape, o.dtype) for s, o in zip(in_specs, operands))
    blk += sum(_nbytes(s.block_shape, o.dtype) for s, o in zip(ospecs, outs))
    scr = sum(_nbytes(s.shape, s.dtype) for s in scratch)
    if semantics is None:
        semantics = ("parallel",) * len(grid)
    return pl.pallas_call(
        body,
        grid=grid,
        in_specs=in_specs,
        out_specs=out_specs,
        out_shape=out_shape,
        scratch_shapes=list(scratch),
        compiler_params=pltpu.CompilerParams(dimension_semantics=semantics,
                                             vmem_limit_bytes=_vmem_limit(blk, scr)),
        name=name,
    )(*operands)


def _const_spec(shape):
    nd = len(shape)
    return pl.BlockSpec(shape, lambda *_: (0,) * nd, pipeline_mode=pl.Buffered(1))


def _norm_body(x_ref, g_ref, o_ref):
    x = x_ref[...]
    ms = jnp.mean(x * x, axis=-1, keepdims=True)
    o_ref[...] = (x * lax.rsqrt(ms + EPS) * g_ref[...]).astype(o_ref.dtype)


def _rms_norm(x, g, tm, name):
    n, d = x.shape
    return _call(_norm_body, (n // tm,),
                 [pl.BlockSpec((tm, d), lambda i: (i, 0)), _const_spec((1, d))],
                 pl.BlockSpec((tm, d), lambda i: (i, 0)),
                 jax.ShapeDtypeStruct((n, d), BF16), (x, g.reshape(1, d)), name)


def _store_head_norm(z, g_ref, o_ref, head_dim):
    lane = lax.broadcasted_iota(jnp.int32, (z.shape[0], LANES), 1)
    for c0 in range(0, z.shape[1], LANES):
        blk = z[:, c0:c0 + LANES]
        sq = blk * blk
        if head_dim == LANES:
            inv = lax.rsqrt(jnp.mean(sq, axis=-1, keepdims=True) + EPS)
        else:
            lo = lane < head_dim
            s_lo = jnp.sum(jnp.where(lo, sq, 0.0), axis=-1, keepdims=True)
            s_hi = jnp.sum(jnp.where(lo, 0.0, sq), axis=-1, keepdims=True)
            inv = jnp.where(lo, lax.rsqrt(s_lo / head_dim + EPS), lax.rsqrt(s_hi / head_dim + EPS))
        o_ref[:, c0:c0 + LANES] = blk * inv * g_ref[:, c0:c0 + LANES]


def _proj_norm_body(n_ref, w_ref, g_ref, o_ref, *, head_dim):
    _store_head_norm(_mm(n_ref[...], w_ref[...]), g_ref, o_ref, head_dim)


def _proj_plain_body(n_ref, w_ref, o_ref):
    o_ref[...] = _mm(n_ref[...], w_ref[...])


def _project(n, w, gain, head_dim, tm, name):
    rows, d = n.shape
    cols = w.shape[1]
    in_specs = [pl.BlockSpec((tm, d), lambda i: (i, 0)), _const_spec((d, cols))]
    operands = [n, w]
    if gain is None:
        body = _proj_plain_body
    else:
        body = functools.partial(_proj_norm_body, head_dim=head_dim)
        in_specs.append(_const_spec((1, cols)))
        operands.append(jnp.tile(gain.reshape(1, head_dim), (1, cols // head_dim)))
    return _call(body, (rows // tm,), in_specs, pl.BlockSpec((tm, cols), lambda i: (i, 0)),
                 jax.ShapeDtypeStruct((rows, cols), F32), operands, name)


def _gate_body(n_ref, w_ref, gav_ref, ws_ref, bs_ref, ya_ref, va_ref, *, cs):
    z = _mm(n_ref[...], w_ref[...])
    uv = 0.5 * z * (1.0 + lax.erf(z * INV_SQRT2))
    tm = z.shape[0]
    for g in range(A_GROUPS):
        c0 = g * A_DIM
        u = uv[:, c0:c0 + A_DIM]
        v = uv[:, A_WIDTH + c0:A_WIDTH + c0 + A_DIM]
        va = v * lax.rsqrt(jnp.mean(v * v, axis=-1, keepdims=True) + EPS) * gav_ref[:, c0:c0 + A_DIM]
        va_ref[:, c0:c0 + A_DIM] = va
        for r0 in range(0, tm, cs):
            mixed = _mm(ws_ref[g], va[r0:r0 + cs]) + bs_ref[:, g:g + 1]
            ya_ref[r0:r0 + cs, c0:c0 + A_DIM] = u[r0:r0 + cs] * mixed


def _gmlp(n, w_uv, g_av, ws_masked, bs_col, tm, name):
    rows, d = n.shape
    cs = ws_masked.shape[1]
    out = jax.ShapeDtypeStruct((rows, A_WIDTH), F32)
    spec = pl.BlockSpec((tm, A_WIDTH), lambda i: (i, 0))
    return _call(functools.partial(_gate_body, cs=cs), (rows // tm,),
                 [pl.BlockSpec((tm, d), lambda i: (i, 0)), _const_spec((d, 2 * A_WIDTH)),
                  _const_spec((1, A_WIDTH)), _const_spec((A_GROUPS, cs, cs)), _const_spec((cs, A_GROUPS))],
                 [spec, spec], [out, out],
                 (n, w_uv, g_av.reshape(1, A_WIDTH), ws_masked, bs_col), name)


def _mem_kv_body(n_ref, w_ref, g_ref, mk_ref, mv_ref):
    kv = _mm(n_ref[...], w_ref[...])
    _store_head_norm(kv[:, :M_WIDTH], g_ref, mk_ref, M_HEAD_DIM)
    mv_ref[...] = kv[:, M_WIDTH:]


def _memory_kv(n_mem, w_kv, g_km):
    rows, d = n_mem.shape
    out = jax.ShapeDtypeStruct((rows, M_WIDTH), F32)
    spec = pl.BlockSpec((N_MEM, M_WIDTH), lambda i: (i, 0))
    return _call(_mem_kv_body, (rows // N_MEM,),
                 [pl.BlockSpec((N_MEM, d), lambda i: (i, 0)), _const_spec((d, 2 * M_WIDTH)),
                  _const_spec((1, M_WIDTH))],
                 [spec, spec], [out, out],
                 (n_mem, w_kv, jnp.tile(g_km.reshape(1, M_HEAD_DIM), (1, M_HEADS))), "memory_kv")


def _memory_attend_heads(qm, mk_ref, mv_ref, ym_ref):
    for h in range(M_HEADS):
        c0 = h * M_HEAD_DIM
        s = _mm(qm[:, c0:c0 + M_HEAD_DIM], mk_ref[:, c0:c0 + M_HEAD_DIM], _NT) * M_SCALE
        p = jnp.exp(s - jnp.max(s, axis=-1, keepdims=True))
        p = p / jnp.sum(p, axis=-1, keepdims=True)
        ym_ref[:, c0:c0 + M_HEAD_DIM] = _mm(p, mv_ref[:, c0:c0 + M_HEAD_DIM])


def _qm_attn_body(n_ref, w_ref, g_ref, mk_ref, mv_ref, ym_ref, qm_ref):
    _store_head_norm(_mm(n_ref[...], w_ref[...]), g_ref, qm_ref, M_HEAD_DIM)
    _memory_attend_heads(qm_ref[...], mk_ref, mv_ref, ym_ref)


def _memory_branch_prompt(n, w_qm, g_qm, mk, mv, seq, tm):
    rows, d = n.shape
    per_seq = seq // tm
    return _call(_qm_attn_body, (rows // tm,),
                 [pl.BlockSpec((tm, d), lambda i: (i, 0)), _const_spec((d, M_WIDTH)), _const_spec((1, M_WIDTH)),
                  pl.BlockSpec((N_MEM, M_WIDTH), lambda i: (i // per_seq, 0)),
                  pl.BlockSpec((N_MEM, M_WIDTH), lambda i: (i // per_seq, 0))],
                 pl.BlockSpec((tm, M_WIDTH), lambda i: (i, 0)),
                 jax.ShapeDtypeStruct((rows, M_WIDTH), F32),
                 (n, w_qm, jnp.tile(g_qm.reshape(1, M_HEAD_DIM), (1, M_HEADS)), mk, mv), "memory_branch_prompt",
                 scratch=[pltpu.VMEM((tm, M_WIDTH), F32)])


def _t5_bucket_np(dist):
    max_exact = N_BUCKETS // 2
    d = np.maximum(dist, 0)
    df = np.maximum(d, 1).astype(np.float64)
    large = max_exact + (np.log(df / max_exact) / math.log(MAX_DISTANCE / max_exact)
                         * (N_BUCKETS - max_exact)).astype(np.int32)
    large = np.minimum(large, N_BUCKETS - 1)
    return np.where(d < max_exact, d, large)


def _prompt_bias_tables(rel_bias):
    qi = np.arange(KEYS_BACK)[:, None] + KEYS_BACK
    kj = np.arange(2 * KEYS_BACK)[None, :]
    delta = qi - kj
    valid = (delta >= 0) & (delta <= KEYS_BACK)
    tabs = []
    for _, dil in DILATIONS:
        bias = jnp.take(rel_bias.astype(F32), jnp.asarray(_t5_bucket_np(delta * dil)), axis=0)
        tabs.append(jnp.where(valid[None], jnp.transpose(bias, (2, 0, 1)), NEG_INF))
    return jnp.stack(tabs, axis=0)


def _sample_bias_tables(rel_bias, n_cache, ds):
    rows = np.arange(n_cache + ds)[:, None]
    dist = n_cache + np.arange(ds)[None, :] - rows
    bias = jnp.take(rel_bias.astype(F32), jnp.asarray(_t5_bucket_np(dist)), axis=0)
    bias = jnp.transpose(bias, (0, 2, 1)).reshape(n_cache + ds, B_HEADS * ds)
    tabs = []
    for window, dil in DILATIONS:
        used = (dist >= 0) & (dist % dil == 0) & (dist <= window)
        used = np.broadcast_to(used[:, None, :], (n_cache + ds, B_HEADS, ds)).reshape(n_cache + ds, B_HEADS * ds)
        tabs.append(jnp.where(used, bias, NEG_INF))
    return jnp.stack(tabs, axis=0)


def _dilated_prompt_body(q_ref, k_ref, v_ref, b_ref, o_ref, od_ref, lse_ref, *, seq):
    lane = lax.broadcasted_iota(jnp.int32, (KEYS_BACK, LANES), 1)
    lo = lane < B_HEAD_DIM

    def rows_of(start, n, dil):
        return pl.ds(start, n) if dil == 1 else pl.ds(start, n, stride=dil)

    def unit(di, dil, start, kstart, nk):
        qv = q_ref[rows_of(start, KEYS_BACK, dil), :] * B_SCALE
        kb = k_ref[rows_of(kstart, nk, dil), :].astype(BF16)
        vb = v_ref[rows_of(kstart, nk, dil), :].astype(BF16)
        outs, lses = [], []
        for h in range(2):
            qh = jnp.where(lo if h == 0 else jnp.logical_not(lo), qv, 0.0).astype(BF16)
            s = lax.dot_general(qh, kb, _NT, preferred_element_type=F32)
            s = s + b_ref[di, h, :, 2 * KEYS_BACK - nk:]
            m = jnp.max(s, axis=-1, keepdims=True)
            p = jnp.exp(s - m)
            l = jnp.sum(p, axis=-1, keepdims=True)
            outs.append(jnp.dot((p * (1.0 / l)).astype(BF16), vb, preferred_element_type=F32))
            lses.append(m + jnp.log(l))
        rows = rows_of(start, KEYS_BACK, dil)
        od_ref[di, rows, :] = jnp.where(lo, outs[0], outs[1])
        lse_ref[di, rows, :] = jnp.where(lo, lses[0], lses[1])

    for di, (_, dil) in enumerate(DILATIONS):
        span = KEYS_BACK * dil
        shift = _log2(dil)

        def first_block(r, c, di=di, dil=dil):
            unit(di, dil, r, r, KEYS_BACK)
            return c

        def later_block(u, c, di=di, dil=dil, span=span, shift=shift):
            start = (1 + (u >> shift)) * span + (u & (dil - 1))
            unit(di, dil, start, start - span, 2 * KEYS_BACK)
            return c

        lax.fori_loop(0, dil, first_block, 0)
        lax.fori_loop(0, seq // KEYS_BACK - dil, later_block, 0)

    top = lse_ref[0]
    for di in range(1, N_DIL):
        top = jnp.maximum(top, lse_ref[di])
    ws = [jnp.exp(lse_ref[di] - top) for di in range(N_DIL)]
    den = ws[0]
    for w in ws[1:]:
        den = den + w
    out = (ws[0] / den) * od_ref[0]
    for di in range(1, N_DIL):
        out = out + (ws[di] / den) * od_ref[di]
    o_ref[...] = out


def _dilated_attention_prompt(q, k, v, bias_tabs, batch, seq):
    pairs = B_WIDTH // LANES
    blk = pl.BlockSpec((seq, LANES), lambda b, p: (b, p))
    return _call(functools.partial(_dilated_prompt_body, seq=seq), (batch, pairs),
                 [blk, blk, blk, pl.BlockSpec((N_DIL, 2, KEYS_BACK, 2 * KEYS_BACK), lambda b, p: (0, p, 0, 0))],
                 blk, jax.ShapeDtypeStruct((batch * seq, B_WIDTH), F32), (q, k, v, bias_tabs),
                 "dilated_attention_prompt",
                 scratch=[pltpu.VMEM((N_DIL, seq, LANES), F32)] * 2)


def _sample_attn_body(q_ref, kn_ref, vn_ref, ck_ref, cv_ref, tc_ref, tn_ref, qm_ref, mk_ref, mv_ref,
                      yb_ref, ym_ref, *, ds):
    cols = B_HEADS * ds
    row = lax.broadcasted_iota(jnp.int32, (cols, B_WIDTH), 0)
    col = lax.broadcasted_iota(jnp.int32, (cols, B_WIDTH), 1)
    own = (row >> _log2(ds)) == (col >> _log2(B_HEAD_DIM))
    q = q_ref[...] * B_SCALE
    qbd = jnp.where(own, jnp.concatenate([q] * B_HEADS, axis=0), 0.0).astype(BF16)
    s_c = _mm(ck_ref[...], qbd, _NT)
    s_n = _mm(kn_ref[...], qbd, _NT)
    eye = lax.broadcasted_iota(jnp.int32, (cols, cols), 0) == lax.broadcasted_iota(jnp.int32, (cols, cols), 1)
    pc, pn, lses = [], [], []
    for di in range(N_DIL):
        sc = s_c + tc_ref[di]
        sn = s_n + tn_ref[di]
        m = jnp.maximum(jnp.max(sc, axis=0, keepdims=True), jnp.max(sn, axis=0, keepdims=True))
        ec = jnp.exp(sc - m)
        en = jnp.exp(sn - m)
        l = jnp.sum(ec, axis=0, keepdims=True) + jnp.sum(en, axis=0, keepdims=True)
        inv = 1.0 / l
        pc.append((ec * inv).astype(BF16))
        pn.append((en * inv).astype(BF16))
        lses.append(m + jnp.log(l))
    o = (_mm(jnp.concatenate(pc, axis=1), cv_ref[...], _TN)
         + _mm(jnp.concatenate(pn, axis=1), vn_ref[...], _TN))
    top = jnp.maximum(jnp.maximum(lses[0], lses[1]), lses[2])
    ws = [jnp.exp(x - top) for x in lses]
    den = ws[0] + ws[1] + ws[2]
    mixed = None
    for di in range(N_DIL):
        w_row = jnp.broadcast_to(ws[di] / den, (cols, cols))
        w_col = jnp.sum(jnp.where(eye, w_row, 0.0), axis=1, keepdims=True)
        term = w_col * o[di * cols:(di + 1) * cols]
        mixed = term if mixed is None else mixed + term
    mixed = jnp.where(own, mixed, 0.0)
    yb = mixed[0:ds]
    for h in range(1, B_HEADS):
        yb = yb + mixed[h * ds:(h + 1) * ds]
    yb_ref[...] = yb
    _memory_attend_heads(qm_ref[...], mk_ref, mv_ref, ym_ref)


def _attention_sample(q, k_new, v_new, cache_k, cache_v, tabs, qm, cache_mk, cache_mv, batch, ds):
    n_cache = cache_k.shape[0] // batch
    cols = B_HEADS * ds
    assert N_DIL == 3 and cols == LANES
    new = pl.BlockSpec((ds, B_WIDTH), lambda b: (b, 0))
    cache = pl.BlockSpec((n_cache, B_WIDTH), lambda b: (b, 0))
    mem = pl.BlockSpec((N_MEM, M_WIDTH), lambda b: (b, 0))
    qm_spec = pl.BlockSpec((ds, M_WIDTH), lambda b: (b, 0))
    return _call(functools.partial(_sample_attn_body, ds=ds), (batch,),
                 [new, new, new, cache, cache, _const_spec((N_DIL, n_cache, cols)), _const_spec((N_DIL, ds, cols)),
                  qm_spec, mem, mem],
                 [new, qm_spec],
                 [jax.ShapeDtypeStruct((batch * ds, B_WIDTH), F32), jax.ShapeDtypeStruct((batch * ds, M_WIDTH), F32)],
                 (q, k_new, v_new, cache_k, cache_v, tabs[:, :n_cache], tabs[:, n_cache:], qm, cache_mk, cache_mv),
                 "attention_sample")


def _out_router_body(ya_ref, yb_ref, ym_ref, x_ref, wo_ref, gf_ref, wr_ref, h_ref, n2_ref, gate_ref):
    o_b = A_WIDTH
    o_m = A_WIDTH + B_WIDTH
    mix = (_mm(ya_ref[...], wo_ref[0:o_b, :]) + _mm(yb_ref[...], wo_ref[o_b:o_m, :])
           + _mm(ym_ref[...], wo_ref[o_m:, :]))
    h = x_ref[...] + mix
    h_ref[...] = h
    n2 = (h * lax.rsqrt(jnp.mean(h * h, axis=-1, keepdims=True) + EPS) * gf_ref[...]).astype(BF16)
    n2_ref[...] = n2
    logits = _mm(n2, wr_ref[...])
    lane_i = lax.broadcasted_iota(jnp.int32, logits.shape, 1)
    lane = lane_i.astype(F32)
    big = float(LANES)

    def first_argmax(vals):
        top = jnp.max(vals, axis=-1, keepdims=True)
        return top, jnp.min(jnp.where(vals == top, lane, big), axis=-1, keepdims=True)

    is_group = (lane_i >= N_EXPERTS) & (lane_i < N_EXPERTS + N_GROUPS)
    _, g_lane = first_argmax(jnp.where(is_group, logits, NEG_INF))
    grp = g_lane.astype(jnp.int32) - N_EXPERTS
    in_grp = (lane_i < N_EXPERTS) & ((lane_i >> _log2(EXPERTS_PER_GROUP)) == grp)
    e_log = jnp.where(in_grp, logits, NEG_INF)
    v1, i1 = first_argmax(e_log)
    v2, i2 = first_argmax(jnp.where(lane == i1, NEG_INF, e_log))
    e2 = jnp.exp(v2 - v1)
    den = 1.0 + e2
    gate_ref[...] = jnp.where(lane == i1, 1.0 / den, 0.0) + jnp.where(lane == i2, e2 / den, 0.0)


def _out_and_router(ya, yb, ym, x, w_out, g_ffn, w_router, tm, name):
    rows, d = x.shape

    def rs(width):
        return pl.BlockSpec((tm, width), lambda i: (i, 0))

    return _call(_out_router_body, (rows // tm,),
                 [rs(A_WIDTH), rs(B_WIDTH), rs(M_WIDTH), rs(d), _const_spec((d, d)), _const_spec((1, d)),
                  _const_spec((d, LANES))],
                 [rs(d), rs(d), rs(LANES)],
                 [jax.ShapeDtypeStruct((rows, d), F32), jax.ShapeDtypeStruct((rows, d), BF16),
                  jax.ShapeDtypeStruct((rows, LANES), F32)],
                 (ya, yb, ym, x, w_out, g_ffn.reshape(1, d), w_router), name)


def _moe_dense_body(n2_ref, gate_ref, h_ref, wg_ref, wu_ref, wd_ref, o_ref):
    e = pl.program_id(1)

    @pl.when(e == 0)
    def _():
        o_ref[...] = h_ref[...]

    n2 = n2_ref[...]
    hg = jnp.dot(n2, wg_ref[0], preferred_element_type=F32)
    hu = jnp.dot(n2, wu_ref[0], preferred_element_type=F32)
    lane = lax.broadcasted_iota(jnp.int32, gate_ref.shape, 1)
    g = jnp.sum(jnp.where(lane == e, gate_ref[...], 0.0), axis=-1, keepdims=True)
    act = hg * jax.nn.sigmoid(hg) * hu * g
    o_ref[...] += jnp.dot(act.astype(BF16), wd_ref[0], preferred_element_type=F32)


def _moe_dense(n2, gates, h, w_gate, w_up, w_down, tm, name):
    rows, d = h.shape
    ff = w_gate.shape[2]
    tok = lambda width: pl.BlockSpec((tm, width), lambda i, e: (i, 0))
    return _call(_moe_dense_body, (rows // tm, N_EXPERTS),
                 [tok(d), tok(LANES), tok(d),
                  pl.BlockSpec((1, d, ff), lambda i, e: (e, 0, 0)), pl.BlockSpec((1, d, ff), lambda i, e: (e, 0, 0)),
                  pl.BlockSpec((1, ff, d), lambda i, e: (e, 0, 0))],
                 tok(d), jax.ShapeDtypeStruct((rows, d), F32), (n2, gates, h, w_gate, w_up, w_down), name,
                 semantics=("parallel", "arbitrary"))


def kernel(x_prompt, x_sample, mem_prompt, cache_win_k, cache_win_v, cache_mem_k, cache_mem_v, rel_bias, g_mix, w_in, g_av, w_s, b_s, g_qb, g_kb, g_qm, g_km, g_mem, w_mem_kv, w_out, g_ffn, w_router_group, w_router_expert, w_gate, w_up, w_down):
    batch, seq, d = x_prompt.shape
    dec_batch, ds, _ = x_sample.shape
    depth = w_in.shape[0]
    n_cache = cache_win_k.shape[2]
    assert seq % (KEYS_BACK * DILATIONS[-1][1]) == 0 and seq <= DILATIONS[-1][0] and seq % CHUNK == 0
    assert n_cache >= DILATIONS[-1][0] and ds <= CHUNK
    o_b = 2 * A_WIDTH
    o_k = o_b + B_WIDTH
    o_v = o_k + B_WIDTH
    o_m = o_v + B_WIDTH

    xp = x_prompt.reshape(batch * seq, d)
    xs = x_sample.reshape(dec_batch * ds, d)
    mem = mem_prompt.reshape(batch * N_MEM, d)
    tm = 512
    rows_s = dec_batch * ds

    bias_prompt = _prompt_bias_tables(rel_bias)
    bias_sample = _sample_bias_tables(rel_bias, n_cache, ds)
    causal = np.tril(np.ones((CHUNK, CHUNK), bool))

    outs = [[] for _ in range(8)]
    for l in range(depth):
        w_in_b = w_in[l].astype(BF16)
        w_out_b = w_out[l].astype(BF16)
        w_router = jnp.concatenate(
            [jnp.transpose(w_router_expert[l], (1, 0, 2)).reshape(d, N_EXPERTS), w_router_group[l],
             jnp.zeros((d, LANES - N_EXPERTS - N_GROUPS), F32)], axis=1).astype(BF16)
        wg_b, wu_b, wd_b = w_gate[l].astype(BF16), w_up[l].astype(BF16), w_down[l].astype(BF16)
        ws_prompt = jnp.where(causal, w_s[l], 0.0)
        bs_prompt = b_s[l].T
        ws_small = jnp.where(causal[:ds, :ds], w_s[l][:, :ds, :ds], 0.0)
        ws_sample = jnp.einsum("ab,gts->gatbs", jnp.eye(dec_batch, dtype=F32), ws_small)
        ws_sample = ws_sample.reshape(A_GROUPS, rows_s, rows_s)
        bs_sample = jnp.tile(b_s[l][:, :ds].T, (dec_batch, 1))

        n = _rms_norm(xp, g_mix[l], tm, "norm_prompt")
        ya, va = _gmlp(n, w_in_b[:, :o_b], g_av[l], ws_prompt, bs_prompt, tm, "gmlp_prompt")
        qb = _project(n, w_in_b[:, o_b:o_k], g_qb[l], B_HEAD_DIM, tm, "q_prompt")
        kb = _project(n, w_in_b[:, o_k:o_v], g_kb[l], B_HEAD_DIM, tm, "k_prompt")
        vb = _project(n, w_in_b[:, o_v:o_m], None, None, tm, "v_prompt")
        n_mem = _rms_norm(mem, g_mem[l], N_MEM, "norm_memory")
        mk, mv = _memory_kv(n_mem, w_mem_kv[l].astype(BF16), g_km[l])
        ym = _memory_branch_prompt(n, w_in_b[:, o_m:], g_qm[l], mk, mv, seq, tm)
        yb = _dilated_attention_prompt(qb, kb, vb, bias_prompt, batch, seq)
        h, n2, gates = _out_and_router(ya, yb, ym, xp, w_out_b, g_ffn[l], w_router, 256, "out_router_prompt")
        xp = _moe_dense(n2, gates, h, wg_b, wu_b, wd_b, tm, "moe_prompt")
        wp = min(DILATIONS[-1][0], seq)
        outs[0].append(kb.reshape(batch, seq, B_HEADS, B_HEAD_DIM)[:, seq - wp:])
        outs[1].append(vb.reshape(batch, seq, B_HEADS, B_HEAD_DIM)[:, seq - wp:])
        outs[2].append(va.reshape(batch, seq, A_GROUPS, A_DIM)[:, seq - CHUNK:])
        outs[3].append(mk.reshape(batch, N_MEM, M_HEADS, M_HEAD_DIM))
        outs[4].append(mv.reshape(batch, N_MEM, M_HEADS, M_HEAD_DIM))

        ns = _rms_norm(xs, g_mix[l], rows_s, "norm_sample")
        ya, va = _gmlp(ns, w_in_b[:, :o_b], g_av[l], ws_sample, bs_sample, rows_s, "gmlp_sample")
        qb = _project(ns, w_in_b[:, o_b:o_k], g_qb[l], B_HEAD_DIM, rows_s, "q_sample")
        kb = _project(ns, w_in_b[:, o_k:o_v], g_kb[l], B_HEAD_DIM, rows_s, "k_sample")
        vb = _project(ns, w_in_b[:, o_v:o_m], None, None, rows_s, "v_sample")
        qm = _project(ns, w_in_b[:, o_m:], g_qm[l], M_HEAD_DIM, rows_s, "qm_sample")
        yb, ym = _attention_sample(
            qb, kb, vb, cache_win_k[l].astype(F32).reshape(dec_batch * n_cache, B_WIDTH),
            cache_win_v[l].astype(F32).reshape(dec_batch * n_cache, B_WIDTH), bias_sample, qm,
            cache_mem_k[l].reshape(dec_batch * N_MEM, M_WIDTH), cache_mem_v[l].reshape(dec_batch * N_MEM, M_WIDTH),
            dec_batch, ds)
        h, n2, gates = _out_and_router(ya, yb, ym, xs, w_out_b, g_ffn[l], w_router, rows_s, "out_router_sample")
        xs = _moe_dense(n2, gates, h, wg_b, wu_b, wd_b, rows_s, "moe_sample")
        outs[5].append(kb.reshape(dec_batch, ds, B_HEADS, B_HEAD_DIM))
        outs[6].append(vb.reshape(dec_batch, ds, B_HEADS, B_HEAD_DIM))
        outs[7].append(va.reshape(dec_batch, ds, A_GROUPS, A_DIM))

    stacked = [jnp.stack(o, axis=0) for o in outs]
    return (xp.reshape(batch, seq, d), xs.reshape(dec_batch, ds, d), *stacked)
```

```python
import functools
import math

import numpy as np
import jax
import jax.numpy as jnp
from jax import lax
from jax.experimental import pallas as pl
from jax.experimental.pallas import tpu as pltpu

F32 = jnp.float32
BF16 = jnp.bfloat16

D_MODEL = 2048
A_GROUPS = 4
A_DIM = 128
A_WIDTH = A_GROUPS * A_DIM
CHUNK = 128
B_HEADS = 16
B_HEAD_DIM = 64
B_WIDTH = B_HEADS * B_HEAD_DIM
M_HEADS = 4
M_HEAD_DIM = 128
M_WIDTH = M_HEADS * M_HEAD_DIM
N_MEM = 256
DILATIONS = ((128, 1), (512, 4), (2048, 16))
N_DIL = len(DILATIONS)
KEYS_BACK = 128
RES = DILATIONS[1][1]
UNITS_IN_FLIGHT = 4
N_BUCKETS = 32
MAX_DISTANCE = 2048
N_GROUPS = 4
EXPERTS_PER_GROUP = 8
N_EXPERTS = N_GROUPS * EXPERTS_PER_GROUP
EPS = 1e-6

LANES = 128
V7X_VMEM_BYTES = 64 * 1024 * 1024
VMEM_CAP = V7X_VMEM_BYTES - 6 * 1024 * 1024
INV_SQRT2 = 0.7071067811865476
B_SCALE = B_HEAD_DIM ** -0.5
M_SCALE = M_HEAD_DIM ** -0.5
NEG_INF = float("-inf")

_NT = (((1,), (1,)), ((), ()))
_TN = (((0,), (0,)), ((), ()))


def _log2(n):
    assert n > 0 and n & (n - 1) == 0, n
    return n.bit_length() - 1


def _mm(a, b, dims=None):
    a, b = a.astype(BF16), b.astype(BF16)
    if dims is None:
        return jnp.dot(a, b, preferred_element_type=F32)
    return lax.dot_general(a, b, dims, preferred_element_type=F32)


def _vmem_limit(block_bytes, scratch_bytes=0):
    est = 2 * block_bytes + scratch_bytes + 16 * 1024 * 1024
    return int(min(max(est, 32 * 1024 * 1024), VMEM_CAP))


def _nbytes(shape, dtype):
    return int(np.prod(shape)) * jnp.dtype(dtype).itemsize


def _call(body, grid, in_specs, out_specs, out_shape, operands, name, semantics=None, scratch=()):
    outs = out_shape if isinstance(out_shape, (list, tuple)) else [out_shape]
    ospecs = out_specs if isinstance(out_specs, (list, tuple)) else [out_specs]
    blk = sum(_nbytes(s.block_shape, o.dtype) for s, o in zip(in_specs, operands))
    blk += sum(_nbytes(s.block_shape, o.dtype) for s, o in zip(ospecs, outs))
    scr = sum(_nbytes(s.shape, s.dtype) for s in scratch)
    if semantics is None:
        semantics = ("parallel",) * len(grid)
    return pl.pallas_call(
        body,
        grid=grid,
        in_specs=in_specs,
        out_specs=out_specs,
        out_shape=out_shape,
        scratch_shapes=list(scratch),
        compiler_params=pltpu.CompilerParams(dimension_semantics=semantics,
                                             vmem_limit_bytes=_vmem_limit(blk, scr)),
        name=name,
    )(*operands)


def _const_spec(shape):
    nd = len(shape)
    return pl.BlockSpec(shape, lambda *_: (0,) * nd, pipeline_mode=pl.Buffered(1))


def _norm_body(x_ref, g_ref, o_ref):
    x = x_ref[...]
    ms = jnp.mean(x * x, axis=-1, keepdims=True)
    o_ref[...] = (x * lax.rsqrt(ms + EPS) * g_ref[...]).astype(o_ref.dtype)


def _rms_norm(x, g, tm, name):
    n, d = x.shape
    return _call(_norm_body, (n // tm,),
                 [pl.BlockSpec((tm, d), lambda i: (i, 0)), _const_spec((1, d))],
                 pl.BlockSpec((tm, d), lambda i: (i, 0)),
                 jax.ShapeDtypeStruct((n, d), BF16), (x, g.reshape(1, d)), name)


def _store_head_norm(z, g_ref, o_ref, head_dim):
    lane = lax.broadcasted_iota(jnp.int32, (z.shape[0], LANES), 1)
    for c0 in range(0, z.shape[1], LANES):
        blk = z[:, c0:c0 + LANES]
        sq = blk * blk
        if head_dim == LANES:
            inv = lax.rsqrt(jnp.mean(sq, axis=-1, keepdims=True) + EPS)
        else:
            lo = lane < head_dim
            s_lo = jnp.sum(jnp.where(lo, sq, 0.0), axis=-1, keepdims=True)
            s_hi = jnp.sum(jnp.where(lo, 0.0, sq), axis=-1, keepdims=True)
            inv = jnp.where(lo, lax.rsqrt(s_lo / head_dim + EPS), lax.rsqrt(s_hi / head_dim + EPS))
        o_ref[:, c0:c0 + LANES] = blk * inv * g_ref[:, c0:c0 + LANES]


def _proj_norm_body(n_ref, w_ref, g_ref, o_ref, *, head_dim):
    _store_head_norm(_mm(n_ref[...], w_ref[...]), g_ref, o_ref, head_dim)


def _proj_plain_body(n_ref, w_ref, o_ref):
    o_ref[...] = _mm(n_ref[...], w_ref[...])


def _project(n, w, gain, head_dim, tm, name):
    rows, d = n.shape
    cols = w.shape[1]
    in_specs = [pl.BlockSpec((tm, d), lambda i: (i, 0)), _const_spec((d, cols))]
    operands = [n, w]
    if gain is None:
        body = _proj_plain_body
    else:
        body = functools.partial(_proj_norm_body, head_dim=head_dim)
        in_specs.append(_const_spec((1, cols)))
        operands.append(jnp.tile(gain.reshape(1, head_dim), (1, cols // head_dim)))
    return _call(body, (rows // tm,), in_specs, pl.BlockSpec((tm, cols), lambda i: (i, 0)),
                 jax.ShapeDtypeStruct((rows, cols), F32), operands, name)


def _gate_body(n_ref, w_ref, gav_ref, ws_ref, bs_ref, ya_ref, va_ref, *, cs):
    z = _mm(n_ref[...], w_ref[...])
    uv = 0.5 * z * (1.0 + lax.erf(z * INV_SQRT2))
    tm = z.shape[0]
    for g in range(A_GROUPS):
        c0 = g * A_DIM
        u = uv[:, c0:c0 + A_DIM]
        v = uv[:, A_WIDTH + c0:A_WIDTH + c0 + A_DIM]
        va = v * lax.rsqrt(jnp.mean(v * v, axis=-1, keepdims=True) + EPS) * gav_ref[:, c0:c0 + A_DIM]
        va_ref[:, c0:c0 + A_DIM] = va
        for r0 in range(0, tm, cs):
            mixed = _mm(ws_ref[g], va[r0:r0 + cs]) + bs_ref[:, g:g + 1]
            ya_ref[r0:r0 + cs, c0:c0 + A_DIM] = u[r0:r0 + cs] * mixed


def _gmlp(n, w_uv, g_av, ws_masked, bs_col, tm, name):
    rows, d = n.shape
    cs = ws_masked.shape[1]
    out = jax.ShapeDtypeStruct((rows, A_WIDTH), F32)
    spec = pl.BlockSpec((tm, A_WIDTH), lambda i: (i, 0))
    return _call(functools.partial(_gate_body, cs=cs), (rows // tm,),
                 [pl.BlockSpec((tm, d), lambda i: (i, 0)), _const_spec((d, 2 * A_WIDTH)),
                  _const_spec((1, A_WIDTH)), _const_spec((A_GROUPS, cs, cs)), _const_spec((cs, A_GROUPS))],
                 [spec, spec], [out, out],
                 (n, w_uv, g_av.reshape(1, A_WIDTH), ws_masked, bs_col), name)


def _mem_kv_body(n_ref, w_ref, g_ref, mk_ref, mv_ref):
    kv = _mm(n_ref[...], w_ref[...])
    _store_head_norm(kv[:, :M_WIDTH], g_ref, mk_ref, M_HEAD_DIM)
    mv_ref[...] = kv[:, M_WIDTH:]


def _memory_kv(n_mem, w_kv, g_km):
    rows, d = n_mem.shape
    out = jax.ShapeDtypeStruct((rows, M_WIDTH), F32)
    spec = pl.BlockSpec((N_MEM, M_WIDTH), lambda i: (i, 0))
    return _call(_mem_kv_body, (rows // N_MEM,),
                 [pl.BlockSpec((N_MEM, d), lambda i: (i, 0)), _const_spec((d, 2 * M_WIDTH)),
                  _const_spec((1, M_WIDTH))],
                 [spec, spec], [out, out],
                 (n_mem, w_kv, jnp.tile(g_km.reshape(1, M_HEAD_DIM), (1, M_HEADS))), "memory_kv")


def _memory_attend_heads(qm, mk_ref, mv_ref, ym_ref):
    for h in range(M_HEADS):
        c0 = h * M_HEAD_DIM
        s = _mm(qm[:, c0:c0 + M_HEAD_DIM], mk_ref[:, c0:c0 + M_HEAD_DIM], _NT) * M_SCALE
        p = jnp.exp(s - jnp.max(s, axis=-1, keepdims=True))
        p = p / jnp.sum(p, axis=-1, keepdims=True)
        ym_ref[:, c0:c0 + M_HEAD_DIM] = _mm(p, mv_ref[:, c0:c0 + M_HEAD_DIM])


def _qm_attn_body(n_ref, w_ref, g_ref, mk_ref, mv_ref, ym_ref, qm_ref):
    _store_head_norm(_mm(n_ref[...], w_ref[...]), g_ref, qm_ref, M_HEAD_DIM)
    _memory_attend_heads(qm_ref[...], mk_ref, mv_ref, ym_ref)


def _memory_branch_prompt(n, w_qm, g_qm, mk, mv, seq, tm):
    rows, d = n.shape
    per_seq = seq // tm
    return _call(_qm_attn_body, (rows // tm,),
                 [pl.BlockSpec((tm, d), lambda i: (i, 0)), _const_spec((d, M_WIDTH)), _const_spec((1, M_WIDTH)),
                  pl.BlockSpec((N_MEM, M_WIDTH), lambda i: (i // per_seq, 0)),
                  pl.BlockSpec((N_MEM, M_WIDTH), lambda i: (i // per_seq, 0))],
                 pl.BlockSpec((tm, M_WIDTH), lambda i: (i, 0)),
                 jax.ShapeDtypeStruct((rows, M_WIDTH), F32),
                 (n, w_qm, jnp.tile(g_qm.reshape(1, M_HEAD_DIM), (1, M_HEADS)), mk, mv), "memory_branch_prompt",
                 scratch=[pltpu.VMEM((tm, M_WIDTH), F32)])


def _t5_bucket_np(dist):
    max_exact = N_BUCKETS // 2
    d = np.maximum(dist, 0)
    df = np.maximum(d, 1).astype(np.float64)
    large = max_exact + (np.log(df / max_exact) / math.log(MAX_DISTANCE / max_exact)
                         * (N_BUCKETS - max_exact)).astype(np.int32)
    large = np.minimum(large, N_BUCKETS - 1)
    return np.where(d < max_exact, d, large)


def _bias_by_distance(rel_bias, dists):
    onehot = np.zeros((N_BUCKETS, len(dists)), np.float32)
    onehot[_t5_bucket_np(np.asarray(dists)), np.arange(len(dists))] = 1.0
    return jnp.dot(rel_bias.astype(F32).T, jnp.asarray(onehot), precision=lax.Precision.HIGHEST)


def _prompt_bias_tables(rel_bias):
    wrap = 2 * KEYS_BACK + 1
    tabs = []
    for _, dil in DILATIONS:
        by_t = _bias_by_distance(rel_bias, [(KEYS_BACK - t) * dil for t in range(KEYS_BACK + 1)])
        row = jnp.concatenate([by_t, jnp.full((B_HEADS, wrap - KEYS_BACK - 1), NEG_INF, F32)], axis=1)
        flat = jnp.tile(row, (1, KEYS_BACK))[:, :KEYS_BACK * 2 * KEYS_BACK]
        tabs.append(flat.reshape(B_HEADS, KEYS_BACK, 2 * KEYS_BACK))
    return jnp.stack(tabs, axis=0)


def _sample_bias_tables(rel_bias, n_cache, ds):
    rows = np.arange(n_cache + ds)[:, None]
    dist = n_cache + np.arange(ds)[None, :] - rows
    n_rows = n_cache + ds
    by_row = _bias_by_distance(rel_bias, [max(n_cache + ds - 1 - c, 0) for c in range(n_rows + ds - 1)])
    bias = jnp.stack([by_row[:, ds - 1 - i:ds - 1 - i + n_rows] for i in range(ds)], axis=2)
    bias = jnp.transpose(bias, (1, 0, 2)).reshape(n_rows, B_HEADS * ds)
    tabs = []
    for window, dil in DILATIONS:
        used = (dist >= 0) & (dist % dil == 0) & (dist <= window)
        used = np.broadcast_to(used[:, None, :], (n_cache + ds, B_HEADS, ds)).reshape(n_cache + ds, B_HEADS * ds)
        tabs.append(jnp.where(used, bias, NEG_INF))
    return jnp.stack(tabs, axis=0)


def _dilated_prompt_body(q_ref, k_ref, v_ref, b_ref, o_ref, qd_ref, kd_ref, vd_ref, od_ref, lse_ref, *, seq):
    per = seq // RES
    lane = lax.broadcasted_iota(jnp.int32, (KEYS_BACK, LANES), 1)
    lo = lane < B_HEAD_DIM

    for c in range(RES):
        qd_ref[pl.ds(c * per, per), :] = q_ref[pl.ds(c, per, stride=RES), :] * B_SCALE
        kd_ref[pl.ds(c * per, per), :] = k_ref[pl.ds(c, per, stride=RES), :]
        vd_ref[pl.ds(c * per, per), :] = v_ref[pl.ds(c, per, stride=RES), :]

    def attend(units):
        nk = units[0][2].shape[0]
        off = 2 * KEYS_BACK - nk
        scores, biases = [], []
        for di, qv, kv, _, _ in units:
            kb = kv.astype(BF16)
            for h in range(2):
                qh = jnp.where(lo if h == 0 else jnp.logical_not(lo), qv, 0.0).astype(BF16)
                scores.append(lax.dot_general(qh, kb, _NT, preferred_element_type=F32))
                biases.append(b_ref[di, h, :, off:])
        s = jnp.concatenate(scores, axis=0) + jnp.concatenate(biases, axis=0)
        m = jnp.max(s, axis=-1, keepdims=True)
        p = jnp.exp(s - m)
        l = jnp.sum(p, axis=-1, keepdims=True)
        pn = (p * (1.0 / l)).astype(BF16)
        lse = m + jnp.log(l)
        for g, (di, _, _, vv, out_rows) in enumerate(units):
            vb = vv.astype(BF16)
            r0 = 2 * g * KEYS_BACK
            r1 = r0 + KEYS_BACK
            o0 = jnp.dot(pn[r0:r1], vb, preferred_element_type=F32)
            o1 = jnp.dot(pn[r1:r1 + KEYS_BACK], vb, preferred_element_type=F32)
            od_ref[di, out_rows, :] = jnp.where(lo, o0, o1)
            lse_ref[di, out_rows, :] = jnp.where(lo, lse[r0:r1], lse[r1:r1 + KEYS_BACK])

    def loop(n, group, make_unit):
        assert n % group == 0

        def step(t, carry):
            attend([make_unit(t * group + g) for g in range(group)])
            return carry
        lax.fori_loop(0, n // group, step, 0)

    blocks = seq // KEYS_BACK
    sub_blocks = per // KEYS_BACK

    def dense_unit(start):
        rows = pl.ds(start, KEYS_BACK)
        keys = rows if isinstance(start, int) and start == 0 else pl.ds(start - KEYS_BACK, 2 * KEYS_BACK)
        return 0, q_ref[rows, :] * B_SCALE, k_ref[keys, :], v_ref[keys, :], rows

    def mid_unit(start, first):
        rows = pl.ds(start, KEYS_BACK)
        keys = rows if first else pl.ds(start - KEYS_BACK, 2 * KEYS_BACK)
        return 1, qd_ref[rows, :], kd_ref[keys, :], vd_ref[keys, :], rows

    def wide_unit(r):
        rows = pl.ds((r & (RES - 1)) * per + (r >> _log2(RES)), KEYS_BACK, stride=RES)
        return 2, qd_ref[rows, :], kd_ref[rows, :], vd_ref[rows, :], rows

    attend([dense_unit(0)] + [mid_unit(c * per, True) for c in range(RES)])
    loop(blocks - 1, max(f for f in range(1, UNITS_IN_FLIGHT + 1) if (blocks - 1) % f == 0),
         lambda u: dense_unit(pl.multiple_of((u + 1) * KEYS_BACK, KEYS_BACK)))
    loop(RES * (sub_blocks - 1), RES,
         lambda u: mid_unit(pl.multiple_of((u & (RES - 1)) * per + (1 + (u >> _log2(RES))) * KEYS_BACK, KEYS_BACK),
                            False))
    loop(RES * RES, RES, wide_unit)

    for c in range(RES):
        for j in range(sub_blocks):
            nat = pl.ds(c + RES * j * KEYS_BACK, KEYS_BACK, stride=RES)
            grp = pl.ds(c * per + j * KEYS_BACK, KEYS_BACK)
            lses = [lse_ref[0, nat, :], lse_ref[1, grp, :], lse_ref[2, grp, :]]
            outs = [od_ref[0, nat, :], od_ref[1, grp, :], od_ref[2, grp, :]]
            top = jnp.maximum(jnp.maximum(lses[0], lses[1]), lses[2])
            ws = [jnp.exp(x - top) for x in lses]
            den = ws[0] + ws[1] + ws[2]
            o_ref[nat, :] = (ws[0] / den) * outs[0] + (ws[1] / den) * outs[1] + (ws[2] / den) * outs[2]


def _dilated_attention_prompt(q, k, v, bias_tabs, batch, seq):
    assert DILATIONS == ((KEYS_BACK, 1), (KEYS_BACK * RES, RES), (KEYS_BACK * RES * RES, RES * RES))
    assert seq == KEYS_BACK * RES * RES
    pairs = B_WIDTH // LANES
    blk = pl.BlockSpec((seq, LANES), lambda b, p: (b, p))
    return _call(functools.partial(_dilated_prompt_body, seq=seq), (batch, pairs),
                 [blk, blk, blk, pl.BlockSpec((N_DIL, 2, KEYS_BACK, 2 * KEYS_BACK), lambda b, p: (0, p, 0, 0))],
                 blk, jax.ShapeDtypeStruct((batch * seq, B_WIDTH), F32), (q, k, v, bias_tabs),
                 "dilated_attention_prompt",
                 scratch=[pltpu.VMEM((seq, LANES), F32)] * 3 + [pltpu.VMEM((N_DIL, seq, LANES), F32)] * 2)


def _sample_attn_body(q_ref, kn_ref, vn_ref, ck_ref, cv_ref, tc_ref, tn_ref, qm_ref, mk_ref, mv_ref,
                      yb_ref, ym_ref, *, ds):
    cols = B_HEADS * ds
    row = lax.broadcasted_iota(jnp.int32, (cols, B_WIDTH), 0)
    col = lax.broadcasted_iota(jnp.int32, (cols, B_WIDTH), 1)
    own = (row >> _log2(ds)) == (col >> _log2(B_HEAD_DIM))
    q = q_ref[...] * B_SCALE
    qbd = jnp.where(own, jnp.concatenate([q] * B_HEADS, axis=0), 0.0).astype(BF16)
    s_c = _mm(ck_ref[...], qbd, _NT)
    s_n = _mm(kn_ref[...], qbd, _NT)
    eye = lax.broadcasted_iota(jnp.int32, (cols, cols), 0) == lax.broadcasted_iota(jnp.int32, (cols, cols), 1)
    pc, pn, lses = [], [], []
    for di in range(N_DIL):
        sc = s_c + tc_ref[di]
        sn = s_n + tn_ref[di]
        m = jnp.maximum(jnp.max(sc, axis=0, keepdims=True), jnp.max(sn, axis=0, keepdims=True))
        ec = jnp.exp(sc - m)
        en = jnp.exp(sn - m)
        l = jnp.sum(ec, axis=0, keepdims=True) + jnp.sum(en, axis=0, keepdims=True)
        inv = 1.0 / l
        pc.append((ec * inv).astype(BF16))
        pn.append((en * inv).astype(BF16))
        lses.append(m + jnp.log(l))
    o = (_mm(jnp.concatenate(pc, axis=1), cv_ref[...], _TN)
         + _mm(jnp.concatenate(pn, axis=1), vn_ref[...], _TN))
    top = jnp.maximum(jnp.maximum(lses[0], lses[1]), lses[2])
    ws = [jnp.exp(x - top) for x in lses]
    den = ws[0] + ws[1] + ws[2]
    mixed = None
    for di in range(N_DIL):
        w_row = jnp.broadcast_to(ws[di] / den, (cols, cols))
        w_col = jnp.sum(jnp.where(eye, w_row, 0.0), axis=1, keepdims=True)
        term = w_col * o[di * cols:(di + 1) * cols]
        mixed = term if mixed is None else mixed + term
    mixed = jnp.where(own, mixed, 0.0)
    yb = mixed[0:ds]
    for h in range(1, B_HEADS):
        yb = yb + mixed[h * ds:(h + 1) * ds]
    yb_ref[...] = yb
    _memory_attend_heads(qm_ref[...], mk_ref, mv_ref, ym_ref)


def _attention_sample(q, k_new, v_new, cache_k, cache_v, tabs, qm, cache_mk, cache_mv, batch, ds):
    n_cache = cache_k.shape[0] // batch
    cols = B_HEADS * ds
    assert N_DIL == 3 and cols == LANES
    new = pl.BlockSpec((ds, B_WIDTH), lambda b: (b, 0))
    cache = pl.BlockSpec((n_cache, B_WIDTH), lambda b: (b, 0))
    mem = pl.BlockSpec((N_MEM, M_WIDTH), lambda b: (b, 0))
    qm_spec = pl.BlockSpec((ds, M_WIDTH), lambda b: (b, 0))
    return _call(functools.partial(_sample_attn_body, ds=ds), (batch,),
                 [new, new, new, cache, cache, _const_spec((N_DIL, n_cache, cols)), _const_spec((N_DIL, ds, cols)),
                  qm_spec, mem, mem],
                 [new, qm_spec],
                 [jax.ShapeDtypeStruct((batch * ds, B_WIDTH), F32), jax.ShapeDtypeStruct((batch * ds, M_WIDTH), F32)],
                 (q, k_new, v_new, cache_k, cache_v, tabs[:, :n_cache], tabs[:, n_cache:], qm, cache_mk, cache_mv),
                 "attention_sample")


def _out_router_body(ya_ref, yb_ref, ym_ref, x_ref, wo_ref, gf_ref, wr_ref, h_ref, n2_ref, gate_ref):
    o_b = A_WIDTH
    o_m = A_WIDTH + B_WIDTH
    mix = (_mm(ya_ref[...], wo_ref[0:o_b, :]) + _mm(yb_ref[...], wo_ref[o_b:o_m, :])
           + _mm(ym_ref[...], wo_ref[o_m:, :]))
    h = x_ref[...] + mix
    h_ref[...] = h
    n2 = (h * lax.rsqrt(jnp.mean(h * h, axis=-1, keepdims=True) + EPS) * gf_ref[...]).astype(BF16)
    n2_ref[...] = n2
    logits = _mm(n2, wr_ref[...])
    lane_i = lax.broadcasted_iota(jnp.int32, logits.shape, 1)
    lane = lane_i.astype(F32)
    big = float(LANES)

    def first_argmax(vals):
        top = jnp.max(vals, axis=-1, keepdims=True)
        return top, jnp.min(jnp.where(vals == top, lane, big), axis=-1, keepdims=True)

    is_group = (lane_i >= N_EXPERTS) & (lane_i < N_EXPERTS + N_GROUPS)
    _, g_lane = first_argmax(jnp.where(is_group, logits, NEG_INF))
    grp = g_lane.astype(jnp.int32) - N_EXPERTS
    in_grp = (lane_i < N_EXPERTS) & ((lane_i >> _log2(EXPERTS_PER_GROUP)) == grp)
    e_log = jnp.where(in_grp, logits, NEG_INF)
    v1, i1 = first_argmax(e_log)
    v2, i2 = first_argmax(jnp.where(lane == i1, NEG_INF, e_log))
    e2 = jnp.exp(v2 - v1)
    den = 1.0 + e2
    gate_ref[...] = jnp.where(lane == i1, 1.0 / den, 0.0) + jnp.where(lane == i2, e2 / den, 0.0)


def _out_and_router(ya, yb, ym, x, w_out, g_ffn, w_router, tm, name):
    rows, d = x.shape

    def rs(width):
        return pl.BlockSpec((tm, width), lambda i: (i, 0))

    return _call(_out_router_body, (rows // tm,),
                 [rs(A_WIDTH), rs(B_WIDTH), rs(M_WIDTH), rs(d), _const_spec((d, d)), _const_spec((1, d)),
                  _const_spec((d, LANES))],
                 [rs(d), rs(d), rs(LANES)],
                 [jax.ShapeDtypeStruct((rows, d), F32), jax.ShapeDtypeStruct((rows, d), BF16),
                  jax.ShapeDtypeStruct((rows, LANES), F32)],
                 (ya, yb, ym, x, w_out, g_ffn.reshape(1, d), w_router), name)


def _moe_dense_body(n2_ref, gate_ref, h_ref, wg_ref, wu_ref, wd_ref, o_ref):
    e = pl.program_id(1)

    @pl.when(e == 0)
    def _():
        o_ref[...] = h_ref[...]

    n2 = n2_ref[...]
    hg = jnp.dot(n2, wg_ref[0], preferred_element_type=F32)
    hu = jnp.dot(n2, wu_ref[0], preferred_element_type=F32)
    lane = lax.broadcasted_iota(jnp.int32, gate_ref.shape, 1)
    g = jnp.sum(jnp.where(lane == e, gate_ref[...], 0.0), axis=-1, keepdims=True)
    act = hg * jax.nn.sigmoid(hg) * hu * g
    o_ref[...] += jnp.dot(act.astype(BF16), wd_ref[0], preferred_element_type=F32)


def _moe_dense(n2, gates, h, w_gate, w_up, w_down, tm, name):
    rows, d = h.shape
    ff = w_gate.shape[2]
    tok = lambda width: pl.BlockSpec((tm, width), lambda i, e: (i, 0))
    return _call(_moe_dense_body, (rows // tm, N_EXPERTS),
                 [tok(d), tok(LANES), tok(d),
                  pl.BlockSpec((1, d, ff), lambda i, e: (e, 0, 0)), pl.BlockSpec((1, d, ff), lambda i, e: (e, 0, 0)),
                  pl.BlockSpec((1, ff, d), lambda i, e: (e, 0, 0))],
                 tok(d), jax.ShapeDtypeStruct((rows, d), F32), (n2, gates, h, w_gate, w_up, w_down), name,
                 semantics=("parallel", "arbitrary"))


def kernel(x_prompt, x_sample, mem_prompt, cache_win_k, cache_win_v, cache_mem_k, cache_mem_v, rel_bias, g_mix, w_in, g_av, w_s, b_s, g_qb, g_kb, g_qm, g_km, g_mem, w_mem_kv, w_out, g_ffn, w_router_group, w_router_expert, w_gate, w_up, w_down):
    batch, seq, d = x_prompt.shape
    dec_batch, ds, _ = x_sample.shape
    depth = w_in.shape[0]
    n_cache = cache_win_k.shape[2]
    assert seq % (KEYS_BACK * DILATIONS[-1][1]) == 0 and seq <= DILATIONS[-1][0] and seq % CHUNK == 0
    assert n_cache >= DILATIONS[-1][0] and ds <= CHUNK
    o_b = 2 * A_WIDTH
    o_k = o_b + B_WIDTH
    o_v = o_k + B_WIDTH
    o_m = o_v + B_WIDTH

    xp = x_prompt.reshape(batch * seq, d)
    xs = x_sample.reshape(dec_batch * ds, d)
    mem = mem_prompt.reshape(batch * N_MEM, d)
    tm = 512
    rows_s = dec_batch * ds

    bias_prompt = _prompt_bias_tables(rel_bias)
    bias_sample = _sample_bias_tables(rel_bias, n_cache, ds)
    causal = np.tril(np.ones((CHUNK, CHUNK), bool))

    outs = [[] for _ in range(8)]
    for l in range(depth):
        w_in_b = w_in[l].astype(BF16)
        w_out_b = w_out[l].astype(BF16)
        w_router = jnp.concatenate(
            [jnp.transpose(w_router_expert[l], (1, 0, 2)).reshape(d, N_EXPERTS), w_router_group[l],
             jnp.zeros((d, LANES - N_EXPERTS - N_GROUPS), F32)], axis=1).astype(BF16)
        wg_b, wu_b, wd_b = w_gate[l].astype(BF16), w_up[l].astype(BF16), w_down[l].astype(BF16)
        ws_prompt = jnp.where(causal, w_s[l], 0.0)
        bs_prompt = b_s[l].T
        ws_small = jnp.where(causal[:ds, :ds], w_s[l][:, :ds, :ds], 0.0)
        ws_sample = jnp.einsum("ab,gts->gatbs", jnp.eye(dec_batch, dtype=F32), ws_small)
        ws_sample = ws_sample.reshape(A_GROUPS, rows_s, rows_s)
        bs_sample = jnp.tile(b_s[l][:, :ds].T, (dec_batch, 1))

        n = _rms_norm(xp, g_mix[l], tm, "norm_prompt")
        ya, va = _gmlp(n, w_in_b[:, :o_b], g_av[l], ws_prompt, bs_prompt, tm, "gmlp_prompt")
        qb = _project(n, w_in_b[:, o_b:o_k], g_qb[l], B_HEAD_DIM, tm, "q_prompt")
        kb = _project(n, w_in_b[:, o_k:o_v], g_kb[l], B_HEAD_DIM, tm, "k_prompt")
        vb = _project(n, w_in_b[:, o_v:o_m], None, None, tm, "v_prompt")
        n_mem = _rms_norm(mem, g_mem[l], N_MEM, "norm_memory")
        mk, mv = _memory_kv(n_mem, w_mem_kv[l].astype(BF16), g_km[l])
        ym = _memory_branch_prompt(n, w_in_b[:, o_m:], g_qm[l], mk, mv, seq, tm)
        yb = _dilated_attention_prompt(qb, kb, vb, bias_prompt, batch, seq)
        h, n2, gates = _out_and_router(ya, yb, ym, xp, w_out_b, g_ffn[l], w_router, 256, "out_router_prompt")
        xp = _moe_dense(n2, gates, h, wg_b, wu_b, wd_b, tm, "moe_prompt")
        wp = min(DILATIONS[-1][0], seq)
        outs[0].append(kb.reshape(batch, seq, B_HEADS, B_HEAD_DIM)[:, seq - wp:])
        outs[1].append(vb.reshape(batch, seq, B_HEADS, B_HEAD_DIM)[:, seq - wp:])
        outs[2].append(va.reshape(batch, seq, A_GROUPS, A_DIM)[:, seq - CHUNK:])
        outs[3].append(mk.reshape(batch, N_MEM, M_HEADS, M_HEAD_DIM))
        outs[4].append(mv.reshape(batch, N_MEM, M_HEADS, M_HEAD_DIM))

        ns = _rms_norm(xs, g_mix[l], rows_s, "norm_sample")
        ya, va = _gmlp(ns, w_in_b[:, :o_b], g_av[l], ws_sample, bs_sample, rows_s, "gmlp_sample")
        qb = _project(ns, w_in_b[:, o_b:o_k], g_qb[l], B_HEAD_DIM, rows_s, "q_sample")
        kb = _project(ns, w_in_b[:, o_k:o_v], g_kb[l], B_HEAD_DIM, rows_s, "k_sample")
        vb = _project(ns, w_in_b[:, o_v:o_m], None, None, rows_s, "v_sample")
        qm = _project(ns, w_in_b[:, o_m:], g_qm[l], M_HEAD_DIM, rows_s, "qm_sample")
        yb, ym = _attention_sample(
            qb, kb, vb, cache_win_k[l].astype(F32).reshape(dec_batch * n_cache, B_WIDTH),
            cache_win_v[l].astype(F32).reshape(dec_batch * n_cache, B_WIDTH), bias_sample, qm,
            cache_mem_k[l].reshape(dec_batch * N_MEM, M_WIDTH), cache_mem_v[l].reshape(dec_batch * N_MEM, M_WIDTH),
            dec_batch, ds)
        h, n2, gates = _out_and_router(ya, yb, ym, xs, w_out_b, g_ffn[l], w_router, rows_s, "out_router_sample")
        xs = _moe_dense(n2, gates, h, wg_b, wu_b, wd_b, rows_s, "moe_sample")
        outs[5].append(kb.reshape(dec_batch, ds, B_HEADS, B_HEAD_DIM))
        outs[6].append(vb.reshape(dec_batch, ds, B_HEADS, B_HEAD_DIM))
        outs[7].append(va.reshape(dec_batch, ds, A_GROUPS, A_DIM))

    stacked = [jnp.stack(o, axis=0) for o in outs]
    return (xp.reshape(batch, seq, d), xs.reshape(dec_batch, ds, d), *stacked)
```

```python
import functools
import math

import numpy as np
import jax
import jax.numpy as jnp
from jax import lax
from jax.experimental import pallas as pl
from jax.experimental.pallas import tpu as pltpu

F32 = jnp.float32
BF16 = jnp.bfloat16

D_MODEL = 2048
A_GROUPS = 4
A_DIM = 128
A_WIDTH = A_GROUPS * A_DIM
CHUNK = 128
B_HEADS = 16
B_HEAD_DIM = 64
B_WIDTH = B_HEADS * B_HEAD_DIM
M_HEADS = 4
M_HEAD_DIM = 128
M_WIDTH = M_HEADS * M_HEAD_DIM
N_MEM = 256
DILATIONS = ((128, 1), (512, 4), (2048, 16))
N_DIL = len(DILATIONS)
KEYS_BACK = 128
RES = DILATIONS[1][1]
UNITS_IN_FLIGHT = 4
N_BUCKETS = 32
MAX_DISTANCE = 2048
N_GROUPS = 4
EXPERTS_PER_GROUP = 8
N_EXPERTS = N_GROUPS * EXPERTS_PER_GROUP
EPS = 1e-6

LANES = 128
V7X_VMEM_BYTES = 64 * 1024 * 1024
VMEM_CAP = V7X_VMEM_BYTES - 6 * 1024 * 1024
INV_SQRT2 = 0.7071067811865476
B_SCALE = B_HEAD_DIM ** -0.5
M_SCALE = M_HEAD_DIM ** -0.5
NEG_INF = float("-inf")

_NT = (((1,), (1,)), ((), ()))
_TN = (((0,), (0,)), ((), ()))


def _log2(n):
    assert n > 0 and n & (n - 1) == 0, n
    return n.bit_length() - 1


def _mm(a, b, dims=None):
    a, b = a.astype(BF16), b.astype(BF16)
    if dims is None:
        return jnp.dot(a, b, preferred_element_type=F32)
    return lax.dot_general(a, b, dims, preferred_element_type=F32)


def _vmem_limit(block_bytes, scratch_bytes=0):
    est = 2 * block_bytes + scratch_bytes + 16 * 1024 * 1024
    return int(min(max(est, 32 * 1024 * 1024), VMEM_CAP))


def _nbytes(shape, dtype):
    return int(np.prod(shape)) * jnp.dtype(dtype).itemsize


def _call(body, grid, in_specs, out_specs, out_shape, operands, name, semantics=None, scratch=()):
    outs = out_shape if isinstance(out_shape, (list, tuple)) else [out_shape]
    ospecs = out_specs if isinstance(out_specs, (list, tuple)) else [out_specs]
    blk = sum(_nbytes(s.block_shape, o.dtype) for s, o in zip(in_specs, operands))
    blk += sum(_nbytes(s.block_shape, o.dtype) for s, o in zip(ospecs, outs))
    scr = sum(_nbytes(s.shape, s.dtype) for s in scratch)
    if semantics is None:
        semantics = ("parallel",) * len(grid)
    return pl.pallas_call(
        body,
        grid=grid,
        in_specs=in_specs,
        out_specs=out_specs,
        out_shape=out_shape,
        scratch_shapes=list(scratch),
        compiler_params=pltpu.CompilerParams(dimension_semantics=semantics,
                                             vmem_limit_bytes=_vmem_limit(blk, scr)),
        name=name,
    )(*operands)


def _const_spec(shape):
    nd = len(shape)
    return pl.BlockSpec(shape, lambda *_: (0,) * nd, pipeline_mode=pl.Buffered(1))


def _norm_body(x_ref, g_ref, o_ref):
    x = x_ref[...]
    ms = jnp.mean(x * x, axis=-1, keepdims=True)
    o_ref[...] = (x * lax.rsqrt(ms + EPS) * g_ref[...]).astype(o_ref.dtype)


def _rms_norm(x, g, tm, name):
    n, d = x.shape
    return _call(_norm_body, (n // tm,),
                 [pl.BlockSpec((tm, d), lambda i: (i, 0)), _const_spec((1, d))],
                 pl.BlockSpec((tm, d), lambda i: (i, 0)),
                 jax.ShapeDtypeStruct((n, d), BF16), (x, g.reshape(1, d)), name)


def _store_head_norm(z, g_ref, o_ref, head_dim):
    lane = lax.broadcasted_iota(jnp.int32, (z.shape[0], LANES), 1)
    for c0 in range(0, z.shape[1], LANES):
        blk = z[:, c0:c0 + LANES]
        sq = blk * blk
        if head_dim == LANES:
            inv = lax.rsqrt(jnp.mean(sq, axis=-1, keepdims=True) + EPS)
        else:
            lo = lane < head_dim
            s_lo = jnp.sum(jnp.where(lo, sq, 0.0), axis=-1, keepdims=True)
            s_hi = jnp.sum(jnp.where(lo, 0.0, sq), axis=-1, keepdims=True)
            inv = jnp.where(lo, lax.rsqrt(s_lo / head_dim + EPS), lax.rsqrt(s_hi / head_dim + EPS))
        o_ref[:, c0:c0 + LANES] = blk * inv * g_ref[:, c0:c0 + LANES]


def _proj_norm_body(n_ref, w_ref, g_ref, o_ref, *, head_dim):
    _store_head_norm(_mm(n_ref[...], w_ref[...]), g_ref, o_ref, head_dim)


def _proj_plain_body(n_ref, w_ref, o_ref):
    o_ref[...] = _mm(n_ref[...], w_ref[...])


def _project(n, w, gain, head_dim, tm, name):
    rows, d = n.shape
    cols = w.shape[1]
    in_specs = [pl.BlockSpec((tm, d), lambda i: (i, 0)), _const_spec((d, cols))]
    operands = [n, w]
    if gain is None:
        body = _proj_plain_body
    else:
        body = functools.partial(_proj_norm_body, head_dim=head_dim)
        in_specs.append(_const_spec((1, cols)))
        operands.append(jnp.tile(gain.reshape(1, head_dim), (1, cols // head_dim)))
    return _call(body, (rows // tm,), in_specs, pl.BlockSpec((tm, cols), lambda i: (i, 0)),
                 jax.ShapeDtypeStruct((rows, cols), F32), operands, name)


def _gate_body(n_ref, w_ref, gav_ref, ws_ref, bs_ref, ya_ref, va_ref, *, cs):
    z = _mm(n_ref[...], w_ref[...])
    uv = 0.5 * z * (1.0 + lax.erf(z * INV_SQRT2))
    tm = z.shape[0]
    for g in range(A_GROUPS):
        c0 = g * A_DIM
        u = uv[:, c0:c0 + A_DIM]
        v = uv[:, A_WIDTH + c0:A_WIDTH + c0 + A_DIM]
        va = v * lax.rsqrt(jnp.mean(v * v, axis=-1, keepdims=True) + EPS) * gav_ref[:, c0:c0 + A_DIM]
        va_ref[:, c0:c0 + A_DIM] = va
        for r0 in range(0, tm, cs):
            mixed = _mm(ws_ref[g], va[r0:r0 + cs]) + bs_ref[:, g:g + 1]
            ya_ref[r0:r0 + cs, c0:c0 + A_DIM] = u[r0:r0 + cs] * mixed


def _gmlp(n, w_uv, g_av, ws_masked, bs_col, tm, name):
    rows, d = n.shape
    cs = ws_masked.shape[1]
    out = jax.ShapeDtypeStruct((rows, A_WIDTH), F32)
    spec = pl.BlockSpec((tm, A_WIDTH), lambda i: (i, 0))
    return _call(functools.partial(_gate_body, cs=cs), (rows // tm,),
                 [pl.BlockSpec((tm, d), lambda i: (i, 0)), _const_spec((d, 2 * A_WIDTH)),
                  _const_spec((1, A_WIDTH)), _const_spec((A_GROUPS, cs, cs)), _const_spec((cs, A_GROUPS))],
                 [spec, spec], [out, out],
                 (n, w_uv, g_av.reshape(1, A_WIDTH), ws_masked, bs_col), name)


def _mem_kv_body(n_ref, w_ref, g_ref, mk_ref, mv_ref):
    kv = _mm(n_ref[...], w_ref[...])
    _store_head_norm(kv[:, :M_WIDTH], g_ref, mk_ref, M_HEAD_DIM)
    mv_ref[...] = kv[:, M_WIDTH:]


def _memory_kv(n_mem, w_kv, g_km):
    rows, d = n_mem.shape
    out = jax.ShapeDtypeStruct((rows, M_WIDTH), F32)
    spec = pl.BlockSpec((N_MEM, M_WIDTH), lambda i: (i, 0))
    return _call(_mem_kv_body, (rows // N_MEM,),
                 [pl.BlockSpec((N_MEM, d), lambda i: (i, 0)), _const_spec((d, 2 * M_WIDTH)),
                  _const_spec((1, M_WIDTH))],
                 [spec, spec], [out, out],
                 (n_mem, w_kv, jnp.tile(g_km.reshape(1, M_HEAD_DIM), (1, M_HEADS))), "memory_kv")


def _memory_attend_heads(qm, mk_ref, mv_ref, ym_ref):
    for h in range(M_HEADS):
        c0 = h * M_HEAD_DIM
        s = _mm(qm[:, c0:c0 + M_HEAD_DIM], mk_ref[:, c0:c0 + M_HEAD_DIM], _NT) * M_SCALE
        p = jnp.exp(s - jnp.max(s, axis=-1, keepdims=True))
        p = p / jnp.sum(p, axis=-1, keepdims=True)
        ym_ref[:, c0:c0 + M_HEAD_DIM] = _mm(p, mv_ref[:, c0:c0 + M_HEAD_DIM])


def _qm_attn_body(n_ref, w_ref, g_ref, mk_ref, mv_ref, ym_ref, qm_ref):
    _store_head_norm(_mm(n_ref[...], w_ref[...]), g_ref, qm_ref, M_HEAD_DIM)
    _memory_attend_heads(qm_ref[...], mk_ref, mv_ref, ym_ref)


def _memory_branch_prompt(n, w_qm, g_qm, mk, mv, seq, tm):
    rows, d = n.shape
    per_seq = seq // tm
    return _call(_qm_attn_body, (rows // tm,),
                 [pl.BlockSpec((tm, d), lambda i: (i, 0)), _const_spec((d, M_WIDTH)), _const_spec((1, M_WIDTH)),
                  pl.BlockSpec((N_MEM, M_WIDTH), lambda i: (i // per_seq, 0)),
                  pl.BlockSpec((N_MEM, M_WIDTH), lambda i: (i // per_seq, 0))],
                 pl.BlockSpec((tm, M_WIDTH), lambda i: (i, 0)),
                 jax.ShapeDtypeStruct((rows, M_WIDTH), F32),
                 (n, w_qm, jnp.tile(g_qm.reshape(1, M_HEAD_DIM), (1, M_HEADS)), mk, mv), "memory_branch_prompt",
                 scratch=[pltpu.VMEM((tm, M_WIDTH), F32)])


def _t5_bucket_np(dist):
    max_exact = N_BUCKETS // 2
    d = np.maximum(dist, 0)
    df = np.maximum(d, 1).astype(np.float64)
    large = max_exact + (np.log(df / max_exact) / math.log(MAX_DISTANCE / max_exact)
                         * (N_BUCKETS - max_exact)).astype(np.int32)
    large = np.minimum(large, N_BUCKETS - 1)
    return np.where(d < max_exact, d, large)


def _bias_by_distance(rel_bias, dists):
    onehot = np.zeros((N_BUCKETS, len(dists)), np.float32)
    onehot[_t5_bucket_np(np.asarray(dists)), np.arange(len(dists))] = 1.0
    return jnp.dot(rel_bias.astype(F32).T, jnp.asarray(onehot), precision=lax.Precision.HIGHEST)


def _prompt_bias_tables(rel_bias):
    wrap = 2 * KEYS_BACK + 1
    tabs = []
    for _, dil in DILATIONS:
        by_t = _bias_by_distance(rel_bias, [(KEYS_BACK - t) * dil for t in range(KEYS_BACK + 1)])
        row = jnp.concatenate([by_t, jnp.full((B_HEADS, wrap - KEYS_BACK - 1), NEG_INF, F32)], axis=1)
        flat = jnp.tile(row, (1, KEYS_BACK))[:, :KEYS_BACK * 2 * KEYS_BACK]
        tabs.append(flat.reshape(B_HEADS, KEYS_BACK, 2 * KEYS_BACK))
    return jnp.stack(tabs, axis=0)


def _sample_bias_tables(rel_bias, n_cache, ds):
    rows = np.arange(n_cache + ds)[:, None]
    dist = n_cache + np.arange(ds)[None, :] - rows
    n_rows = n_cache + ds
    by_row = _bias_by_distance(rel_bias, [max(n_cache + ds - 1 - c, 0) for c in range(n_rows + ds - 1)])
    bias = jnp.stack([by_row[:, ds - 1 - i:ds - 1 - i + n_rows] for i in range(ds)], axis=2)
    bias = jnp.transpose(bias, (1, 0, 2)).reshape(n_rows, B_HEADS * ds)
    tabs = []
    for window, dil in DILATIONS:
        used = (dist >= 0) & (dist % dil == 0) & (dist <= window)
        used = np.broadcast_to(used[:, None, :], (n_cache + ds, B_HEADS, ds)).reshape(n_cache + ds, B_HEADS * ds)
        tabs.append(jnp.where(used, bias, NEG_INF))
    return jnp.stack(tabs, axis=0)


def _dilated_prompt_body(q_ref, k_ref, v_ref, b_ref, o_ref, qd_ref, kd_ref, vd_ref, od_ref, lse_ref, *, seq):
    per = seq // RES
    lane = lax.broadcasted_iota(jnp.int32, (KEYS_BACK, LANES), 1)
    lo = lane < B_HEAD_DIM

    for c in range(RES):
        qd_ref[pl.ds(c * per, per), :] = q_ref[pl.ds(c, per, stride=RES), :] * B_SCALE
        kd_ref[pl.ds(c * per, per), :] = k_ref[pl.ds(c, per, stride=RES), :]
        vd_ref[pl.ds(c * per, per), :] = v_ref[pl.ds(c, per, stride=RES), :]

    def attend(units):
        nk = units[0][2].shape[0]
        off = 2 * KEYS_BACK - nk
        scores, biases = [], []
        for di, qv, kv, _, _ in units:
            kb = kv.astype(BF16)
            for h in range(2):
                qh = jnp.where(lo if h == 0 else jnp.logical_not(lo), qv, 0.0).astype(BF16)
                scores.append(lax.dot_general(qh, kb, _NT, preferred_element_type=F32))
                biases.append(b_ref[di, h, :, off:])
        s = jnp.concatenate(scores, axis=0) + jnp.concatenate(biases, axis=0)
        m = jnp.max(s, axis=-1, keepdims=True)
        p = jnp.exp(s - m)
        l = jnp.sum(p, axis=-1, keepdims=True)
        pn = (p * (1.0 / l)).astype(BF16)
        lse = m + jnp.log(l)
        for g, (di, _, _, vv, out_rows) in enumerate(units):
            vb = vv.astype(BF16)
            r0 = 2 * g * KEYS_BACK
            r1 = r0 + KEYS_BACK
            o0 = jnp.dot(pn[r0:r1], vb, preferred_element_type=F32)
            o1 = jnp.dot(pn[r1:r1 + KEYS_BACK], vb, preferred_element_type=F32)
            od_ref[di, out_rows, :] = jnp.where(lo, o0, o1)
            lse_ref[di, out_rows, :] = jnp.where(lo, lse[r0:r1], lse[r1:r1 + KEYS_BACK])

    def loop(n, group, make_unit):
        assert n % group == 0

        def step(t, carry):
            attend([make_unit(t * group + g) for g in range(group)])
            return carry
        lax.fori_loop(0, n // group, step, 0)

    blocks = seq // KEYS_BACK
    sub_blocks = per // KEYS_BACK

    def dense_unit(start):
        rows = pl.ds(start, KEYS_BACK)
        keys = rows if isinstance(start, int) and start == 0 else pl.ds(start - KEYS_BACK, 2 * KEYS_BACK)
        return 0, q_ref[rows, :] * B_SCALE, k_ref[keys, :], v_ref[keys, :], rows

    def mid_unit(start, first):
        rows = pl.ds(start, KEYS_BACK)
        keys = rows if first else pl.ds(start - KEYS_BACK, 2 * KEYS_BACK)
        return 1, qd_ref[rows, :], kd_ref[keys, :], vd_ref[keys, :], rows

    def wide_unit(r):
        rows = pl.ds((r & (RES - 1)) * per + (r >> _log2(RES)), KEYS_BACK, stride=RES)
        return 2, qd_ref[rows, :], kd_ref[rows, :], vd_ref[rows, :], rows

    attend([dense_unit(0)] + [mid_unit(c * per, True) for c in range(RES)])
    loop(blocks - 1, max(f for f in range(1, UNITS_IN_FLIGHT + 1) if (blocks - 1) % f == 0),
         lambda u: dense_unit(pl.multiple_of((u + 1) * KEYS_BACK, KEYS_BACK)))
    loop(RES * (sub_blocks - 1), RES,
         lambda u: mid_unit(pl.multiple_of((u & (RES - 1)) * per + (1 + (u >> _log2(RES))) * KEYS_BACK, KEYS_BACK),
                            False))
    loop(RES * RES, RES, wide_unit)

    for c in range(RES):
        for j in range(sub_blocks):
            nat = pl.ds(c + RES * j * KEYS_BACK, KEYS_BACK, stride=RES)
            grp = pl.ds(c * per + j * KEYS_BACK, KEYS_BACK)
            lses = [lse_ref[0, nat, :], lse_ref[1, grp, :], lse_ref[2, grp, :]]
            outs = [od_ref[0, nat, :], od_ref[1, grp, :], od_ref[2, grp, :]]
            top = jnp.maximum(jnp.maximum(lses[0], lses[1]), lses[2])
            ws = [jnp.exp(x - top) for x in lses]
            den = ws[0] + ws[1] + ws[2]
            o_ref[nat, :] = (ws[0] / den) * outs[0] + (ws[1] / den) * outs[1] + (ws[2] / den) * outs[2]


def _dilated_attention_prompt(q, k, v, bias_tabs, batch, seq):
    assert DILATIONS == ((KEYS_BACK, 1), (KEYS_BACK * RES, RES), (KEYS_BACK * RES * RES, RES * RES))
    assert seq == KEYS_BACK * RES * RES
    pairs = B_WIDTH // LANES
    blk = pl.BlockSpec((seq, LANES), lambda b, p: (b, p))
    return _call(functools.partial(_dilated_prompt_body, seq=seq), (batch, pairs),
                 [blk, blk, blk, pl.BlockSpec((N_DIL, 2, KEYS_BACK, 2 * KEYS_BACK), lambda b, p: (0, p, 0, 0))],
                 blk, jax.ShapeDtypeStruct((batch * seq, B_WIDTH), F32), (q, k, v, bias_tabs),
                 "dilated_attention_prompt",
                 scratch=[pltpu.VMEM((seq, LANES), F32)] * 3 + [pltpu.VMEM((N_DIL, seq, LANES), F32)] * 2)


def _sample_attn_body(q_ref, kn_ref, vn_ref, ck_ref, cv_ref, tc_ref, tn_ref, qm_ref, mk_ref, mv_ref,
                      yb_ref, ym_ref, *, ds):
    cols = B_HEADS * ds
    row = lax.broadcasted_iota(jnp.int32, (cols, B_WIDTH), 0)
    col = lax.broadcasted_iota(jnp.int32, (cols, B_WIDTH), 1)
    own = (row >> _log2(ds)) == (col >> _log2(B_HEAD_DIM))
    q = q_ref[...] * B_SCALE
    qbd = jnp.where(own, jnp.concatenate([q] * B_HEADS, axis=0), 0.0).astype(BF16)
    s_c = _mm(ck_ref[...], qbd, _NT)
    s_n = _mm(kn_ref[...], qbd, _NT)
    eye = lax.broadcasted_iota(jnp.int32, (cols, cols), 0) == lax.broadcasted_iota(jnp.int32, (cols, cols), 1)
    pc, pn, lses = [], [], []
    for di in range(N_DIL):
        sc = s_c + tc_ref[di]
        sn = s_n + tn_ref[di]
        m = jnp.maximum(jnp.max(sc, axis=0, keepdims=True), jnp.max(sn, axis=0, keepdims=True))
        ec = jnp.exp(sc - m)
        en = jnp.exp(sn - m)
        l = jnp.sum(ec, axis=0, keepdims=True) + jnp.sum(en, axis=0, keepdims=True)
        inv = 1.0 / l
        pc.append((ec * inv).astype(BF16))
        pn.append((en * inv).astype(BF16))
        lses.append(m + jnp.log(l))
    o = (_mm(jnp.concatenate(pc, axis=1), cv_ref[...], _TN)
         + _mm(jnp.concatenate(pn, axis=1), vn_ref[...], _TN))
    top = jnp.maximum(jnp.maximum(lses[0], lses[1]), lses[2])
    ws = [jnp.exp(x - top) for x in lses]
    den = ws[0] + ws[1] + ws[2]
    mixed = None
    for di in range(N_DIL):
        w_row = jnp.broadcast_to(ws[di] / den, (cols, cols))
        w_col = jnp.sum(jnp.where(eye, w_row, 0.0), axis=1, keepdims=True)
        term = w_col * o[di * cols:(di + 1) * cols]
        mixed = term if mixed is None else mixed + term
    mixed = jnp.where(own, mixed, 0.0)
    yb = mixed[0:ds]
    for h in range(1, B_HEADS):
        yb = yb + mixed[h * ds:(h + 1) * ds]
    yb_ref[...] = yb
    _memory_attend_heads(qm_ref[...], mk_ref, mv_ref, ym_ref)


def _attention_sample(q, k_new, v_new, cache_k, cache_v, tabs, qm, cache_mk, cache_mv, batch, ds):
    n_cache = cache_k.shape[0] // batch
    cols = B_HEADS * ds
    assert N_DIL == 3 and cols == LANES
    new = pl.BlockSpec((ds, B_WIDTH), lambda b: (b, 0))
    cache = pl.BlockSpec((n_cache, B_WIDTH), lambda b: (b, 0))
    mem = pl.BlockSpec((N_MEM, M_WIDTH), lambda b: (b, 0))
    qm_spec = pl.BlockSpec((ds, M_WIDTH), lambda b: (b, 0))
    return _call(functools.partial(_sample_attn_body, ds=ds), (batch,),
                 [new, new, new, cache, cache, _const_spec((N_DIL, n_cache, cols)), _const_spec((N_DIL, ds, cols)),
                  qm_spec, mem, mem],
                 [new, qm_spec],
                 [jax.ShapeDtypeStruct((batch * ds, B_WIDTH), F32), jax.ShapeDtypeStruct((batch * ds, M_WIDTH), F32)],
                 (q, k_new, v_new, cache_k, cache_v, tabs[:, :n_cache], tabs[:, n_cache:], qm, cache_mk, cache_mv),
                 "attention_sample")


def _ffn_norm(h, gf_ref):
    return (h * lax.rsqrt(jnp.mean(h * h, axis=-1, keepdims=True) + EPS) * gf_ref[...]).astype(BF16)


def _out_router_body(ya_ref, yb_ref, ym_ref, x_ref, wo_ref, gf_ref, wr_ref, rec_ref):
    o_b = A_WIDTH
    o_m = A_WIDTH + B_WIDTH
    d = x_ref.shape[1]
    mix = (_mm(ya_ref[...], wo_ref[0:o_b, :]) + _mm(yb_ref[...], wo_ref[o_b:o_m, :])
           + _mm(ym_ref[...], wo_ref[o_m:, :]))
    h = x_ref[...] + mix
    rec_ref[:, :d] = h
    logits = _mm(_ffn_norm(h, gf_ref), wr_ref[...])
    lane_i = lax.broadcasted_iota(jnp.int32, logits.shape, 1)
    lane = lane_i.astype(F32)
    big = float(LANES)

    def first_argmax(vals):
        top = jnp.max(vals, axis=-1, keepdims=True)
        return top, jnp.min(jnp.where(vals == top, lane, big), axis=-1, keepdims=True)

    is_group = (lane_i >= N_EXPERTS) & (lane_i < N_EXPERTS + N_GROUPS)
    _, g_lane = first_argmax(jnp.where(is_group, logits, NEG_INF))
    grp = g_lane.astype(jnp.int32) - N_EXPERTS
    in_grp = (lane_i < N_EXPERTS) & ((lane_i >> _log2(EXPERTS_PER_GROUP)) == grp)
    e_log = jnp.where(in_grp, logits, NEG_INF)
    v1, i1 = first_argmax(e_log)
    v2, i2 = first_argmax(jnp.where(lane == i1, NEG_INF, e_log))
    e2 = jnp.exp(v2 - v1)
    den = 1.0 + e2
    rec_ref[:, d:] = (jnp.where(lane == i1, 1.0 / den, 0.0) + jnp.where(lane == i2, e2 / den, 0.0)
                      + jnp.where(lane_i == N_EXPERTS, grp.astype(F32), 0.0))


def _out_and_router(ya, yb, ym, x, w_out, g_ffn, w_router, tm, name):
    rows, d = x.shape

    def rs(width):
        return pl.BlockSpec((tm, width), lambda i: (i, 0))

    return _call(_out_router_body, (rows // tm,),
                 [rs(A_WIDTH), rs(B_WIDTH), rs(M_WIDTH), rs(d), _const_spec((d, d)), _const_spec((1, d)),
                  _const_spec((d, LANES))],
                 rs(d + LANES), jax.ShapeDtypeStruct((rows, d + LANES), F32),
                 (ya, yb, ym, x, w_out, g_ffn.reshape(1, d), w_router), name)


def _expert_ffn(n2, gates, expert_lane, wg_ref, wu_ref, wd_ref):
    hg = jnp.dot(n2, wg_ref[0], preferred_element_type=F32)
    hu = jnp.dot(n2, wu_ref[0], preferred_element_type=F32)
    lane = lax.broadcasted_iota(jnp.int32, gates.shape, 1)
    g = jnp.sum(jnp.where(lane == expert_lane, gates, 0.0), axis=-1, keepdims=True)
    act = hg * jax.nn.sigmoid(hg) * hu * g
    return jnp.dot(act.astype(BF16), wd_ref[0], preferred_element_type=F32)


def _moe_dense_body(rec_ref, gf_ref, wg_ref, wu_ref, wd_ref, o_ref, n2_ref):
    e = pl.program_id(1)
    d = o_ref.shape[1]

    @pl.when(e == 0)
    def _():
        h = rec_ref[:, :d]
        o_ref[...] = h
        n2_ref[...] = _ffn_norm(h, gf_ref)

    o_ref[...] += _expert_ffn(n2_ref[...], rec_ref[:, d:], e, wg_ref, wu_ref, wd_ref)


def _moe_dense(rec, g_ffn, w_gate, w_up, w_down, tm, name):
    rows = rec.shape[0]
    d, ff = w_gate.shape[1:]
    wspec = lambda a, b: pl.BlockSpec((1, a, b), lambda i, e: (e, 0, 0))
    return _call(_moe_dense_body, (rows // tm, N_EXPERTS),
                 [pl.BlockSpec((tm, d + LANES), lambda i, e: (i, 0)), _const_spec((1, d)),
                  wspec(d, ff), wspec(d, ff), wspec(ff, d)],
                 pl.BlockSpec((tm, d), lambda i, e: (i, 0)), jax.ShapeDtypeStruct((rows, d), F32),
                 (rec, g_ffn.reshape(1, d), w_gate, w_up, w_down), name,
                 semantics=("parallel", "arbitrary"), scratch=[pltpu.VMEM((tm, d), BF16)])


def _group_sort_plan(rec, tm):
    n, width = rec.shape
    d = width - LANES
    n_tiles = n // tm + N_GROUPS
    grp = rec[:, d + N_EXPERTS].astype(jnp.int32)
    order = jnp.argsort(grp, stable=True).astype(jnp.int32)
    counts = jnp.sum((grp[:, None] == jnp.arange(N_GROUPS)[None, :]).astype(jnp.int32), axis=0)
    tiles_per = (counts + tm - 1) // tm
    tile_end = jnp.cumsum(tiles_per)
    tile_start = tile_end - tiles_per
    tok_start = jnp.cumsum(counts) - counts
    n_used = tile_end[-1]
    t = jnp.arange(n_tiles, dtype=jnp.int32)
    tgrp = jnp.minimum(jnp.sum((t[:, None] >= tile_end[None, :]).astype(jnp.int32), axis=1), N_GROUPS - 1)
    row0 = (t - tile_start[tgrp]) * tm
    nvalid = jnp.where(t < n_used, jnp.clip(counts[tgrp] - row0, 0, tm), 0).astype(jnp.int32)
    k = row0[:, None] + jnp.arange(tm, dtype=jnp.int32)[None, :]
    real = jnp.arange(tm, dtype=jnp.int32)[None, :] < nvalid[:, None]
    src = jnp.where(real, order[jnp.clip(tok_start[tgrp][:, None] + k, 0, n - 1)], -1)
    return src.reshape(-1).astype(jnp.int32), tgrp.astype(jnp.int32), nvalid, n_used.reshape(1).astype(jnp.int32)


def _moe_grouped_body(src_ref, tgrp_ref, nvalid_ref, nused_ref, rec_hbm, gf_ref, wg_ref, wu_ref, wd_ref, y_hbm,
                      rec_buf, n2_ref, acc_ref, y_buf, gsem, ssem, *, tm):
    i = pl.program_id(0)
    e = pl.program_id(1)
    d = y_buf.shape[1]
    n_used = nused_ref[0]
    slot = i % 2

    def gather_rows(tile, slot_):
        def row(r, c):
            tok = jnp.maximum(src_ref[tile * tm + r], 0)
            pltpu.make_async_copy(rec_hbm.at[pl.ds(tok, 1)], rec_buf.at[slot_, pl.ds(r, 1)], gsem.at[slot_]).start()
            return c
        lax.fori_loop(0, tm, row, 0, unroll=8)

    def wait_gather(slot_):
        pltpu.make_async_copy(rec_hbm.at[pl.ds(0, tm)], rec_buf.at[slot_], gsem.at[slot_]).wait()

    def scatter_copy(tile, r):
        tok = src_ref[tile * tm + r]
        return pltpu.make_async_copy(y_buf.at[pl.ds(r, 1)], y_hbm.at[pl.ds(tok, 1)], ssem.at[0])

    def scatter_rows(tile, wait):
        def row(r, c):
            cp = scatter_copy(tile, r)
            if wait:
                cp.wait()
            else:
                cp.start()
            return c
        lax.fori_loop(0, nvalid_ref[tile], row, 0)

    @pl.when(i < n_used)
    def _():
        @pl.when(e == 0)
        def _():
            @pl.when(i == 0)
            def _():
                gather_rows(0, 0)

            wait_gather(slot)

            @pl.when(i + 1 < n_used)
            def _():
                gather_rows(i + 1, 1 - slot)

            n2_ref[...] = _ffn_norm(rec_buf[slot, :, :d], gf_ref)
            acc_ref[...] = jnp.zeros_like(acc_ref)

        lane0 = tgrp_ref[i] * EXPERTS_PER_GROUP
        acc_ref[...] += _expert_ffn(n2_ref[...], rec_buf[slot, :, d:], lane0 + e, wg_ref, wu_ref, wd_ref)

        @pl.when(e == EXPERTS_PER_GROUP - 1)
        def _():
            @pl.when(i > 0)
            def _():
                scatter_rows(i - 1, True)

            y_buf[...] = rec_buf[slot, :, :d] + acc_ref[...]
            scatter_rows(i, False)

            @pl.when(i == n_used - 1)
            def _():
                scatter_rows(i, True)


def _moe_grouped(rec, g_ffn, w_gate, w_up, w_down, tm, name):
    n = rec.shape[0]
    d, ff = w_gate.shape[1:]
    src, tgrp, nvalid, n_used = _group_sort_plan(rec, tm)
    n_tiles = tgrp.shape[0]

    def widx(i, e, src_ref, tgrp_ref, nvalid_ref, nused_ref):
        last = nused_ref[0] - 1
        live = i <= last
        return (tgrp_ref[jnp.minimum(i, last)] * EXPERTS_PER_GROUP
                + jnp.where(live, e, EXPERTS_PER_GROUP - 1), 0, 0)

    scratch = [pltpu.VMEM((2, tm, d + LANES), F32), pltpu.VMEM((tm, d), BF16), pltpu.VMEM((tm, d), F32),
               pltpu.VMEM((tm, d), F32), pltpu.SemaphoreType.DMA((2,)), pltpu.SemaphoreType.DMA((1,))]
    blk = 3 * _nbytes((d, ff), BF16) + _nbytes((1, d), F32)
    scr = sum(_nbytes(s.shape, s.dtype) for s in scratch[:4])
    return pl.pallas_call(
        functools.partial(_moe_grouped_body, tm=tm),
        grid_spec=pltpu.PrefetchScalarGridSpec(
            num_scalar_prefetch=4,
            grid=(n_tiles, EXPERTS_PER_GROUP),
            in_specs=[pl.BlockSpec(memory_space=pl.ANY),
                      pl.BlockSpec((1, d), lambda i, e, *_: (0, 0)),
                      pl.BlockSpec((1, d, ff), widx), pl.BlockSpec((1, d, ff), widx), pl.BlockSpec((1, ff, d), widx)],
            out_specs=pl.BlockSpec(memory_space=pl.ANY),
            scratch_shapes=scratch),
        out_shape=jax.ShapeDtypeStruct((n, d), F32),
        compiler_params=pltpu.CompilerParams(dimension_semantics=("arbitrary", "arbitrary"),
                                             vmem_limit_bytes=_vmem_limit(blk, scr)),
        name=name,
    )(src, tgrp, nvalid, n_used, rec, g_ffn.reshape(1, d), w_gate, w_up, w_down)


def kernel(x_prompt, x_sample, mem_prompt, cache_win_k, cache_win_v, cache_mem_k, cache_mem_v, rel_bias, g_mix, w_in, g_av, w_s, b_s, g_qb, g_kb, g_qm, g_km, g_mem, w_mem_kv, w_out, g_ffn, w_router_group, w_router_expert, w_gate, w_up, w_down):
    batch, seq, d = x_prompt.shape
    dec_batch, ds, _ = x_sample.shape
    depth = w_in.shape[0]
    n_cache = cache_win_k.shape[2]
    assert seq % (KEYS_BACK * DILATIONS[-1][1]) == 0 and seq <= DILATIONS[-1][0] and seq % CHUNK == 0
    assert n_cache >= DILATIONS[-1][0] and ds <= CHUNK
    o_b = 2 * A_WIDTH
    o_k = o_b + B_WIDTH
    o_v = o_k + B_WIDTH
    o_m = o_v + B_WIDTH

    xp = x_prompt.reshape(batch * seq, d)
    xs = x_sample.reshape(dec_batch * ds, d)
    mem = mem_prompt.reshape(batch * N_MEM, d)
    tm = 512
    rows_s = dec_batch * ds

    bias_prompt = _prompt_bias_tables(rel_bias)
    bias_sample = _sample_bias_tables(rel_bias, n_cache, ds)
    causal = np.tril(np.ones((CHUNK, CHUNK), bool))

    outs = [[] for _ in range(8)]
    for l in range(depth):
        w_in_b = w_in[l].astype(BF16)
        w_out_b = w_out[l].astype(BF16)
        w_router = jnp.concatenate(
            [jnp.transpose(w_router_expert[l], (1, 0, 2)).reshape(d, N_EXPERTS), w_router_group[l],
             jnp.zeros((d, LANES - N_EXPERTS - N_GROUPS), F32)], axis=1).astype(BF16)
        wg_b, wu_b, wd_b = w_gate[l].astype(BF16), w_up[l].astype(BF16), w_down[l].astype(BF16)
        ws_prompt = jnp.where(causal, w_s[l], 0.0)
        bs_prompt = b_s[l].T
        ws_small = jnp.where(causal[:ds, :ds], w_s[l][:, :ds, :ds], 0.0)
        ws_sample = jnp.einsum("ab,gts->gatbs", jnp.eye(dec_batch, dtype=F32), ws_small)
        ws_sample = ws_sample.reshape(A_GROUPS, rows_s, rows_s)
        bs_sample = jnp.tile(b_s[l][:, :ds].T, (dec_batch, 1))

        n = _rms_norm(xp, g_mix[l], tm, "norm_prompt")
        ya, va = _gmlp(n, w_in_b[:, :o_b], g_av[l], ws_prompt, bs_prompt, tm, "gmlp_prompt")
        qb = _project(n, w_in_b[:, o_b:o_k], g_qb[l], B_HEAD_DIM, tm, "q_prompt")
        kb = _project(n, w_in_b[:, o_k:o_v], g_kb[l], B_HEAD_DIM, tm, "k_prompt")
        vb = _project(n, w_in_b[:, o_v:o_m], None, None, tm, "v_prompt")
        n_mem = _rms_norm(mem, g_mem[l], N_MEM, "norm_memory")
        mk, mv = _memory_kv(n_mem, w_mem_kv[l].astype(BF16), g_km[l])
        ym = _memory_branch_prompt(n, w_in_b[:, o_m:], g_qm[l], mk, mv, seq, tm)
        yb = _dilated_attention_prompt(qb, kb, vb, bias_prompt, batch, seq)
        rec = _out_and_router(ya, yb, ym, xp, w_out_b, g_ffn[l], w_router, 256, "out_router_prompt")
        xp = _moe_grouped(rec, g_ffn[l], wg_b, wu_b, wd_b, tm, "moe_prompt")
        wp = min(DILATIONS[-1][0], seq)
        outs[0].append(kb.reshape(batch, seq, B_HEADS, B_HEAD_DIM)[:, seq - wp:])
        outs[1].append(vb.reshape(batch, seq, B_HEADS, B_HEAD_DIM)[:, seq - wp:])
        outs[2].append(va.reshape(batch, seq, A_GROUPS, A_DIM)[:, seq - CHUNK:])
        outs[3].append(mk.reshape(batch, N_MEM, M_HEADS, M_HEAD_DIM))
        outs[4].append(mv.reshape(batch, N_MEM, M_HEADS, M_HEAD_DIM))

        ns = _rms_norm(xs, g_mix[l], rows_s, "norm_sample")
        ya, va = _gmlp(ns, w_in_b[:, :o_b], g_av[l], ws_sample, bs_sample, rows_s, "gmlp_sample")
        qb = _project(ns, w_in_b[:, o_b:o_k], g_qb[l], B_HEAD_DIM, rows_s, "q_sample")
        kb = _project(ns, w_in_b[:, o_k:o_v], g_kb[l], B_HEAD_DIM, rows_s, "k_sample")
        vb = _project(ns, w_in_b[:, o_v:o_m], None, None, rows_s, "v_sample")
        qm = _project(ns, w_in_b[:, o_m:], g_qm[l], M_HEAD_DIM, rows_s, "qm_sample")
        yb, ym = _attention_sample(
            qb, kb, vb, cache_win_k[l].astype(F32).reshape(dec_batch * n_cache, B_WIDTH),
            cache_win_v[l].astype(F32).reshape(dec_batch * n_cache, B_WIDTH), bias_sample, qm,
            cache_mem_k[l].reshape(dec_batch * N_MEM, M_WIDTH), cache_mem_v[l].reshape(dec_batch * N_MEM, M_WIDTH),
            dec_batch, ds)
        rec = _out_and_router(ya, yb, ym, xs, w_out_b, g_ffn[l], w_router, rows_s, "out_router_sample")
        xs = _moe_dense(rec, g_ffn[l], wg_b, wu_b, wd_b, rows_s, "moe_sample")
        outs[5].append(kb.reshape(dec_batch, ds, B_HEADS, B_HEAD_DIM))
        outs[6].append(vb.reshape(dec_batch, ds, B_HEADS, B_HEAD_DIM))
        outs[7].append(va.reshape(dec_batch, ds, A_GROUPS, A_DIM))

    stacked = [jnp.stack(o, axis=0) for o in outs]
    return (xp.reshape(batch, seq, d), xs.reshape(dec_batch, ds, d), *stacked)
```

```python
import functools
import math

import numpy as np
import jax
import jax.numpy as jnp
from jax import lax
from jax.experimental import pallas as pl
from jax.experimental.pallas import tpu as pltpu

F32 = jnp.float32
BF16 = jnp.bfloat16

D_MODEL = 2048
A_GROUPS = 4
A_DIM = 128
A_WIDTH = A_GROUPS * A_DIM
CHUNK = 128
B_HEADS = 16
B_HEAD_DIM = 64
B_WIDTH = B_HEADS * B_HEAD_DIM
M_HEADS = 4
M_HEAD_DIM = 128
M_WIDTH = M_HEADS * M_HEAD_DIM
N_MEM = 256
DILATIONS = ((128, 1), (512, 4), (2048, 16))
N_DIL = len(DILATIONS)
KEYS_BACK = 128
RES = DILATIONS[1][1]
UNITS_IN_FLIGHT = 4
N_BUCKETS = 32
MAX_DISTANCE = 2048
N_GROUPS = 4
EXPERTS_PER_GROUP = 8
N_EXPERTS = N_GROUPS * EXPERTS_PER_GROUP
EPS = 1e-6

LANES = 128
V7X_VMEM_BYTES = 64 * 1024 * 1024
VMEM_CAP = V7X_VMEM_BYTES - 6 * 1024 * 1024
INV_SQRT2 = 0.7071067811865476
B_SCALE = B_HEAD_DIM ** -0.5
M_SCALE = M_HEAD_DIM ** -0.5
NEG_INF = float("-inf")

_NT = (((1,), (1,)), ((), ()))
_TN = (((0,), (0,)), ((), ()))


def _log2(n):
    assert n > 0 and n & (n - 1) == 0, n
    return n.bit_length() - 1


def _mm(a, b, dims=None):
    a, b = a.astype(BF16), b.astype(BF16)
    if dims is None:
        return jnp.dot(a, b, preferred_element_type=F32)
    return lax.dot_general(a, b, dims, preferred_element_type=F32)


def _vmem_limit(block_bytes, scratch_bytes=0):
    est = 2 * block_bytes + scratch_bytes + 16 * 1024 * 1024
    return int(min(max(est, 32 * 1024 * 1024), VMEM_CAP))


def _nbytes(shape, dtype):
    return int(np.prod(shape)) * jnp.dtype(dtype).itemsize


def _call(body, grid, in_specs, out_specs, out_shape, operands, name, semantics=None, scratch=()):
    outs = out_shape if isinstance(out_shape, (list, tuple)) else [out_shape]
    ospecs = out_specs if isinstance(out_specs, (list, tuple)) else [out_specs]
    blk = sum(_nbytes(s.block_shape, o.dtype) for s, o in zip(in_specs, operands))
    blk += sum(_nbytes(s.block_shape, o.dtype) for s, o in zip(ospecs, outs))
    scr = sum(_nbytes(s.shape, s.dtype) for s in scratch)
    if semantics is None:
        semantics = ("parallel",) * len(grid)
    return pl.pallas_call(
        body,
        grid=grid,
        in_specs=in_specs,
        out_specs=out_specs,
        out_shape=out_shape,
        scratch_shapes=list(scratch),
        compiler_params=pltpu.CompilerParams(dimension_semantics=semantics,
                                             vmem_limit_bytes=_vmem_limit(blk, scr)),
        name=name,
    )(*operands)


def _const_spec(shape):
    nd = len(shape)
    return pl.BlockSpec(shape, lambda *_: (0,) * nd, pipeline_mode=pl.Buffered(1))


def _row_norm(x, g_ref):
    return (x * lax.rsqrt(jnp.mean(x * x, axis=-1, keepdims=True) + EPS) * g_ref[...]).astype(BF16)


def _store_head_norm(z, g_ref, o_ref, head_dim):
    lane = lax.broadcasted_iota(jnp.int32, (z.shape[0], LANES), 1)
    for c0 in range(0, z.shape[1], LANES):
        blk = z[:, c0:c0 + LANES]
        sq = blk * blk
        if head_dim == LANES:
            inv = lax.rsqrt(jnp.mean(sq, axis=-1, keepdims=True) + EPS)
        else:
            lo = lane < head_dim
            s_lo = jnp.sum(jnp.where(lo, sq, 0.0), axis=-1, keepdims=True)
            s_hi = jnp.sum(jnp.where(lo, 0.0, sq), axis=-1, keepdims=True)
            inv = jnp.where(lo, lax.rsqrt(s_lo / head_dim + EPS), lax.rsqrt(s_hi / head_dim + EPS))
        o_ref[:, c0:c0 + LANES] = blk * inv * g_ref[:, c0:c0 + LANES]


def _memory_attend_heads(qm, mk_ref, mv_ref, ym_ref):
    for h in range(M_HEADS):
        c0 = h * M_HEAD_DIM
        if len(mk_ref.shape) == 3:
            mk, mv = mk_ref[:, h, :], mv_ref[:, h, :]
        else:
            mk, mv = mk_ref[:, c0:c0 + M_HEAD_DIM], mv_ref[:, c0:c0 + M_HEAD_DIM]
        s = _mm(qm[:, c0:c0 + M_HEAD_DIM], mk, _NT) * M_SCALE
        p = jnp.exp(s - jnp.max(s, axis=-1, keepdims=True))
        p = p / jnp.sum(p, axis=-1, keepdims=True)
        ym_ref[:, c0:c0 + M_HEAD_DIM] = _mm(p, mv)


def _mixer_inputs_body(*refs, cs, attend_memory):
    (x_ref, gmix_ref, w_ref, gav_ref, ws_ref, bs_ref, gq_ref, gk_ref, gqm_ref), refs = refs[:9], refs[9:]
    if attend_memory:
        (mk_ref, mv_ref), refs = refs[:2], refs[2:]
    ya_ref, va_ref, q_ref, k_ref, v_ref, m_ref, n_ref = refs[:7]
    o_b = 2 * A_WIDTH
    o_k = o_b + B_WIDTH
    o_v = o_k + B_WIDTH
    o_m = o_v + B_WIDTH
    n_ref[...] = _row_norm(x_ref[...], gmix_ref)
    tm = n_ref.shape[0]

    z = _mm(n_ref[...], w_ref[:, :o_b])
    uv = 0.5 * z * (1.0 + lax.erf(z * INV_SQRT2))
    for g in range(A_GROUPS):
        c0 = g * A_DIM
        u = uv[:, c0:c0 + A_DIM]
        v = uv[:, A_WIDTH + c0:A_WIDTH + c0 + A_DIM]
        va = v * lax.rsqrt(jnp.mean(v * v, axis=-1, keepdims=True) + EPS) * gav_ref[:, c0:c0 + A_DIM]
        va_ref[:, c0:c0 + A_DIM] = va
        for r0 in range(0, tm, cs):
            mixed = _mm(ws_ref[g], va[r0:r0 + cs]) + bs_ref[:, g:g + 1]
            ya_ref[r0:r0 + cs, c0:c0 + A_DIM] = u[r0:r0 + cs] * mixed

    _store_head_norm(_mm(n_ref[...], w_ref[:, o_b:o_k]), gq_ref, q_ref, B_HEAD_DIM)
    _store_head_norm(_mm(n_ref[...], w_ref[:, o_k:o_v]), gk_ref, k_ref, B_HEAD_DIM)
    v_ref[...] = _mm(n_ref[...], w_ref[:, o_v:o_m])
    if attend_memory:
        qm_ref = refs[7]
        _store_head_norm(_mm(n_ref[...], w_ref[:, o_m:]), gqm_ref, qm_ref, M_HEAD_DIM)
        _memory_attend_heads(qm_ref[...], mk_ref, mv_ref, m_ref)
    else:
        _store_head_norm(_mm(n_ref[...], w_ref[:, o_m:]), gqm_ref, m_ref, M_HEAD_DIM)


def _mixer_inputs(x, g_mix, w_in, g_av, ws_masked, bs_col, g_qb, g_kb, g_qm, memory, tm, name):
    rows, d = x.shape
    cs = ws_masked.shape[1]

    def tiled(gain, heads):
        return jnp.tile(gain.reshape(1, -1), (1, heads))

    def rs(width):
        return pl.BlockSpec((tm, width), lambda i: (i, 0))

    in_specs = [rs(d), _const_spec((1, d)), _const_spec(w_in.shape), _const_spec((1, A_WIDTH)),
                _const_spec((A_GROUPS, cs, cs)), _const_spec((cs, A_GROUPS)), _const_spec((1, B_WIDTH)),
                _const_spec((1, B_WIDTH)), _const_spec((1, M_WIDTH))]
    operands = [x, g_mix.reshape(1, d), w_in, g_av.reshape(1, A_WIDTH), ws_masked, bs_col,
                tiled(g_qb, B_HEADS), tiled(g_kb, B_HEADS), tiled(g_qm, M_HEADS)]
    scratch = [pltpu.VMEM((tm, d), BF16)]
    if memory is not None:
        mk, mv, seq = memory
        mem = pl.BlockSpec((N_MEM, M_WIDTH), lambda i: (i // (seq // tm), 0))
        in_specs += [mem, mem]
        operands += [mk, mv]
        scratch.append(pltpu.VMEM((tm, M_WIDTH), F32))
    widths = (A_WIDTH, A_WIDTH, B_WIDTH, B_WIDTH, B_WIDTH, M_WIDTH)
    return _call(functools.partial(_mixer_inputs_body, cs=cs, attend_memory=memory is not None), (rows // tm,),
                 in_specs, [rs(w) for w in widths], [jax.ShapeDtypeStruct((rows, w), F32) for w in widths],
                 operands, name, scratch=scratch)


def _mem_kv_body(x_ref, gmem_ref, w_ref, g_ref, mk_ref, mv_ref):
    kv = _mm(_row_norm(x_ref[...], gmem_ref), w_ref[...])
    _store_head_norm(kv[:, :M_WIDTH], g_ref, mk_ref, M_HEAD_DIM)
    mv_ref[...] = kv[:, M_WIDTH:]


def _memory_kv(mem, g_mem, w_kv, g_km):
    rows, d = mem.shape
    out = jax.ShapeDtypeStruct((rows, M_WIDTH), F32)
    spec = pl.BlockSpec((N_MEM, M_WIDTH), lambda i: (i, 0))
    return _call(_mem_kv_body, (rows // N_MEM,),
                 [pl.BlockSpec((N_MEM, d), lambda i: (i, 0)), _const_spec((1, d)), _const_spec((d, 2 * M_WIDTH)),
                  _const_spec((1, M_WIDTH))],
                 [spec, spec], [out, out],
                 (mem, g_mem.reshape(1, d), w_kv, jnp.tile(g_km.reshape(1, M_HEAD_DIM), (1, M_HEADS))), "memory_kv")


def _t5_bucket_np(dist):
    max_exact = N_BUCKETS // 2
    d = np.maximum(dist, 0)
    df = np.maximum(d, 1).astype(np.float64)
    large = max_exact + (np.log(df / max_exact) / math.log(MAX_DISTANCE / max_exact)
                         * (N_BUCKETS - max_exact)).astype(np.int32)
    large = np.minimum(large, N_BUCKETS - 1)
    return np.where(d < max_exact, d, large)


def _bias_by_distance(rel_bias, dists):
    onehot = np.zeros((N_BUCKETS, len(dists)), np.float32)
    onehot[_t5_bucket_np(np.asarray(dists)), np.arange(len(dists))] = 1.0
    return jnp.dot(rel_bias.astype(F32).T, jnp.asarray(onehot), precision=lax.Precision.HIGHEST)


def _prompt_bias_tables(rel_bias):
    wrap = 2 * KEYS_BACK + 1
    tabs = []
    for _, dil in DILATIONS:
        by_t = _bias_by_distance(rel_bias, [(KEYS_BACK - t) * dil for t in range(KEYS_BACK + 1)])
        row = jnp.concatenate([by_t, jnp.full((B_HEADS, wrap - KEYS_BACK - 1), NEG_INF, F32)], axis=1)
        flat = jnp.tile(row, (1, KEYS_BACK))[:, :KEYS_BACK * 2 * KEYS_BACK]
        tabs.append(flat.reshape(B_HEADS, KEYS_BACK, 2 * KEYS_BACK))
    return jnp.stack(tabs, axis=0)


def _sample_bias_tables(rel_bias, n_cache, ds):
    rows = np.arange(n_cache + ds)[:, None]
    dist = n_cache + np.arange(ds)[None, :] - rows
    n_rows = n_cache + ds
    by_row = _bias_by_distance(rel_bias, [max(n_cache + ds - 1 - c, 0) for c in range(n_rows + ds - 1)])
    bias = jnp.stack([by_row[:, ds - 1 - i:ds - 1 - i + n_rows] for i in range(ds)], axis=2)
    bias = jnp.transpose(bias, (1, 0, 2)).reshape(n_rows, B_HEADS * ds)
    tabs = []
    for window, dil in DILATIONS:
        used = (dist >= 0) & (dist % dil == 0) & (dist <= window)
        used = np.broadcast_to(used[:, None, :], (n_cache + ds, B_HEADS, ds)).reshape(n_cache + ds, B_HEADS * ds)
        tabs.append(jnp.where(used, bias, NEG_INF))
    return jnp.stack(tabs, axis=0)


def _dilated_prompt_body(q_ref, k_ref, v_ref, b_ref, o_ref, qd_ref, kd_ref, vd_ref, od_ref, lse_ref, *, seq):
    per = seq // RES
    lane = lax.broadcasted_iota(jnp.int32, (KEYS_BACK, LANES), 1)
    lo = lane < B_HEAD_DIM

    for c in range(RES):
        qd_ref[pl.ds(c * per, per), :] = q_ref[pl.ds(c, per, stride=RES), :] * B_SCALE
        kd_ref[pl.ds(c * per, per), :] = k_ref[pl.ds(c, per, stride=RES), :]
        vd_ref[pl.ds(c * per, per), :] = v_ref[pl.ds(c, per, stride=RES), :]

    def attend(units):
        nk = units[0][2].shape[0]
        off = 2 * KEYS_BACK - nk
        scores, biases = [], []
        for di, qv, kv, _, _ in units:
            kb = kv.astype(BF16)
            for h in range(2):
                qh = jnp.where(lo if h == 0 else jnp.logical_not(lo), qv, 0.0).astype(BF16)
                scores.append(lax.dot_general(qh, kb, _NT, preferred_element_type=F32))
                biases.append(b_ref[di, h, :, off:])
        s = jnp.concatenate(scores, axis=0) + jnp.concatenate(biases, axis=0)
        m = jnp.max(s, axis=-1, keepdims=True)
        p = jnp.exp(s - m)
        l = jnp.sum(p, axis=-1, keepdims=True)
        pn = (p * (1.0 / l)).astype(BF16)
        lse = m + jnp.log(l)
        for g, (di, _, _, vv, out_rows) in enumerate(units):
            vb = vv.astype(BF16)
            r0 = 2 * g * KEYS_BACK
            r1 = r0 + KEYS_BACK
            o0 = jnp.dot(pn[r0:r1], vb, preferred_element_type=F32)
            o1 = jnp.dot(pn[r1:r1 + KEYS_BACK], vb, preferred_element_type=F32)
            od_ref[di, out_rows, :] = jnp.where(lo, o0, o1)
            lse_ref[di, out_rows, :] = jnp.where(lo, lse[r0:r1], lse[r1:r1 + KEYS_BACK])

    def loop(n, group, make_unit):
        assert n % group == 0

        def step(t, carry):
            attend([make_unit(t * group + g) for g in range(group)])
            return carry
        lax.fori_loop(0, n // group, step, 0)

    blocks = seq // KEYS_BACK
    sub_blocks = per // KEYS_BACK

    def dense_unit(start):
        rows = pl.ds(start, KEYS_BACK)
        keys = rows if isinstance(start, int) and start == 0 else pl.ds(start - KEYS_BACK, 2 * KEYS_BACK)
        return 0, q_ref[rows, :] * B_SCALE, k_ref[keys, :], v_ref[keys, :], rows

    def mid_unit(start, first):
        rows = pl.ds(start, KEYS_BACK)
        keys = rows if first else pl.ds(start - KEYS_BACK, 2 * KEYS_BACK)
        return 1, qd_ref[rows, :], kd_ref[keys, :], vd_ref[keys, :], rows

    def wide_unit(r):
        rows = pl.ds((r & (RES - 1)) * per + (r >> _log2(RES)), KEYS_BACK, stride=RES)
        return 2, qd_ref[rows, :], kd_ref[rows, :], vd_ref[rows, :], rows

    attend([dense_unit(0)] + [mid_unit(c * per, True) for c in range(RES)])
    loop(blocks - 1, max(f for f in range(1, UNITS_IN_FLIGHT + 1) if (blocks - 1) % f == 0),
         lambda u: dense_unit(pl.multiple_of((u + 1) * KEYS_BACK, KEYS_BACK)))
    loop(RES * (sub_blocks - 1), RES,
         lambda u: mid_unit(pl.multiple_of((u & (RES - 1)) * per + (1 + (u >> _log2(RES))) * KEYS_BACK, KEYS_BACK),
                            False))
    loop(RES * RES, RES, wide_unit)

    for c in range(RES):
        for j in range(sub_blocks):
            nat = pl.ds(c + RES * j * KEYS_BACK, KEYS_BACK, stride=RES)
            grp = pl.ds(c * per + j * KEYS_BACK, KEYS_BACK)
            lses = [lse_ref[0, nat, :], lse_ref[1, grp, :], lse_ref[2, grp, :]]
            outs = [od_ref[0, nat, :], od_ref[1, grp, :], od_ref[2, grp, :]]
            top = jnp.maximum(jnp.maximum(lses[0], lses[1]), lses[2])
            ws = [jnp.exp(x - top) for x in lses]
            den = ws[0] + ws[1] + ws[2]
            o_ref[nat, :] = (ws[0] / den) * outs[0] + (ws[1] / den) * outs[1] + (ws[2] / den) * outs[2]


def _dilated_attention_prompt(q, k, v, bias_tabs, batch, seq):
    assert DILATIONS == ((KEYS_BACK, 1), (KEYS_BACK * RES, RES), (KEYS_BACK * RES * RES, RES * RES))
    assert seq == KEYS_BACK * RES * RES
    pairs = B_WIDTH // LANES
    blk = pl.BlockSpec((seq, LANES), lambda b, p: (b, p))
    return _call(functools.partial(_dilated_prompt_body, seq=seq), (batch, pairs),
                 [blk, blk, blk, pl.BlockSpec((N_DIL, 2, KEYS_BACK, 2 * KEYS_BACK), lambda b, p: (0, p, 0, 0))],
                 blk, jax.ShapeDtypeStruct((batch * seq, B_WIDTH), F32), (q, k, v, bias_tabs),
                 "dilated_attention_prompt",
                 scratch=[pltpu.VMEM((seq, LANES), F32)] * 3 + [pltpu.VMEM((N_DIL, seq, LANES), F32)] * 2)


def _sample_attn_body(q_ref, kn_ref, vn_ref, kf_ref, vf_ref, kc_ref, vc_ref, tf_ref, tc_ref, tn_ref,
                      qm_ref, mk_ref, mv_ref, yb_ref, ym_ref, *, ds):
    cols = B_HEADS * ds
    row = lax.broadcasted_iota(jnp.int32, (cols, B_WIDTH), 0)
    col = lax.broadcasted_iota(jnp.int32, (cols, B_WIDTH), 1)
    own = (row >> _log2(ds)) == (col >> _log2(B_HEAD_DIM))
    q = q_ref[...] * B_SCALE
    qbd = jnp.where(own, jnp.concatenate([q] * B_HEADS, axis=0), 0.0).astype(BF16)
    s_f = _mm(kf_ref[...], qbd, _NT) + tf_ref[...]
    s_c = _mm(kc_ref[...], qbd, _NT)
    s_n = _mm(kn_ref[...], qbd, _NT)
    eye = lax.broadcasted_iota(jnp.int32, (cols, cols), 0) == lax.broadcasted_iota(jnp.int32, (cols, cols), 1)
    pc, pn, lses = [], [], []
    p_far = None
    for di in range(N_DIL):
        widest = di == N_DIL - 1
        sc = s_c + tc_ref[di]
        sn = s_n + tn_ref[di]
        m = jnp.maximum(jnp.max(sc, axis=0, keepdims=True), jnp.max(sn, axis=0, keepdims=True))
        if widest:
            m = jnp.maximum(m, jnp.max(s_f, axis=0, keepdims=True))
        ec = jnp.exp(sc - m)
        en = jnp.exp(sn - m)
        l = jnp.sum(ec, axis=0, keepdims=True) + jnp.sum(en, axis=0, keepdims=True)
        if widest:
            ef = jnp.exp(s_f - m)
            l = l + jnp.sum(ef, axis=0, keepdims=True)
        inv = 1.0 / l
        if widest:
            p_far = (ef * inv).astype(BF16)
        pc.append((ec * inv).astype(BF16))
        pn.append((en * inv).astype(BF16))
        lses.append(m + jnp.log(l))
    o = (_mm(jnp.concatenate(pc, axis=1), vc_ref[...], _TN)
         + _mm(jnp.concatenate(pn, axis=1), vn_ref[...], _TN))
    o_far = _mm(p_far, vf_ref[...], _TN)
    o = jnp.concatenate([o[:(N_DIL - 1) * cols], o[(N_DIL - 1) * cols:] + o_far], axis=0)
    top = jnp.maximum(jnp.maximum(lses[0], lses[1]), lses[2])
    ws = [jnp.exp(x - top) for x in lses]
    den = ws[0] + ws[1] + ws[2]
    mixed = None
    for di in range(N_DIL):
        w_row = jnp.broadcast_to(ws[di] / den, (cols, cols))
        w_col = jnp.sum(jnp.where(eye, w_row, 0.0), axis=1, keepdims=True)
        term = w_col * o[di * cols:(di + 1) * cols]
        mixed = term if mixed is None else mixed + term
    mixed = jnp.where(own, mixed, 0.0)
    yb = mixed[0:ds]
    for h in range(1, B_HEADS):
        yb = yb + mixed[h * ds:(h + 1) * ds]
    yb_ref[...] = yb
    _memory_attend_heads(qm_ref[...], mk_ref, mv_ref, ym_ref)


def _attention_sample(q, k_new, v_new, cache_k, cache_v, tabs, qm, cache_mk, cache_mv, ds):
    batch, n_cache = cache_k.shape[:2]
    cols = B_HEADS * ds
    wide = DILATIONS[-1][1]
    far = n_cache - DILATIONS[-2][0]
    assert N_DIL == 3 and cols == LANES and far % wide == 0 and ds <= wide
    n_far = far // wide * ds
    n_near = n_cache - far

    def split_rows(x, lead):
        a = len(lead)
        xf = x[(slice(None),) * a + (slice(0, far),)].reshape(*lead, far // wide, wide, *x.shape[a + 1:])
        xf = xf[(slice(None),) * (a + 1) + (slice(0, ds),)].reshape(*lead, n_far, *x.shape[a + 1:])
        return xf, x[(slice(None),) * a + (slice(far, n_cache),)]

    def flat(x):
        return x.astype(BF16).reshape(batch * x.shape[1], B_WIDTH)

    kf, kc = (flat(x) for x in split_rows(cache_k, (batch,)))
    vf, vc = (flat(x) for x in split_rows(cache_v, (batch,)))
    tf, tc = split_rows(tabs[:, :n_cache], (N_DIL,))
    new = pl.BlockSpec((ds, B_WIDTH), lambda b: (b, 0))
    far_spec = pl.BlockSpec((n_far, B_WIDTH), lambda b: (b, 0))
    near_spec = pl.BlockSpec((n_near, B_WIDTH), lambda b: (b, 0))
    mem = pl.BlockSpec((N_MEM, M_HEADS, M_HEAD_DIM), lambda b: (b, 0, 0))
    cache_mk = cache_mk.reshape(batch * N_MEM, M_HEADS, M_HEAD_DIM)
    cache_mv = cache_mv.reshape(batch * N_MEM, M_HEADS, M_HEAD_DIM)
    qm_spec = pl.BlockSpec((ds, M_WIDTH), lambda b: (b, 0))
    return _call(functools.partial(_sample_attn_body, ds=ds), (batch,),
                 [new, new, new, far_spec, far_spec, near_spec, near_spec, _const_spec((n_far, cols)),
                  _const_spec((N_DIL, n_near, cols)), _const_spec((N_DIL, ds, cols)), qm_spec, mem, mem],
                 [new, qm_spec],
                 [jax.ShapeDtypeStruct((batch * ds, B_WIDTH), F32), jax.ShapeDtypeStruct((batch * ds, M_WIDTH), F32)],
                 (q, k_new, v_new, kf, vf, kc, vc, tf[N_DIL - 1], tc, tabs[:, n_cache:], qm, cache_mk, cache_mv),
                 "attention_sample")


def _out_router_body(ya_ref, yb_ref, ym_ref, x_ref, wo_ref, gf_ref, wr_ref, rec_ref):
    o_b = A_WIDTH
    o_m = A_WIDTH + B_WIDTH
    d = x_ref.shape[1]
    mix = (_mm(ya_ref[...], wo_ref[0:o_b, :]) + _mm(yb_ref[...], wo_ref[o_b:o_m, :])
           + _mm(ym_ref[...], wo_ref[o_m:, :]))
    h = x_ref[...] + mix
    rec_ref[:, :d] = h
    logits = _mm(_row_norm(h, gf_ref), wr_ref[...])
    lane_i = lax.broadcasted_iota(jnp.int32, logits.shape, 1)
    lane = lane_i.astype(F32)
    big = float(LANES)

    def first_argmax(vals):
        top = jnp.max(vals, axis=-1, keepdims=True)
        return top, jnp.min(jnp.where(vals == top, lane, big), axis=-1, keepdims=True)

    is_group = (lane_i >= N_EXPERTS) & (lane_i < N_EXPERTS + N_GROUPS)
    _, g_lane = first_argmax(jnp.where(is_group, logits, NEG_INF))
    grp = g_lane.astype(jnp.int32) - N_EXPERTS
    in_grp = (lane_i < N_EXPERTS) & ((lane_i >> _log2(EXPERTS_PER_GROUP)) == grp)
    e_log = jnp.where(in_grp, logits, NEG_INF)
    v1, i1 = first_argmax(e_log)
    v2, i2 = first_argmax(jnp.where(lane == i1, NEG_INF, e_log))
    e2 = jnp.exp(v2 - v1)
    den = 1.0 + e2
    rec_ref[:, d:] = (jnp.where(lane == i1, 1.0 / den, 0.0) + jnp.where(lane == i2, e2 / den, 0.0)
                      + jnp.where(lane_i == N_EXPERTS, grp.astype(F32), 0.0))


def _out_and_router(ya, yb, ym, x, w_out, g_ffn, w_router, tm, name):
    rows, d = x.shape

    def rs(width):
        return pl.BlockSpec((tm, width), lambda i: (i, 0))

    return _call(_out_router_body, (rows // tm,),
                 [rs(A_WIDTH), rs(B_WIDTH), rs(M_WIDTH), rs(d), _const_spec((d, d)), _const_spec((1, d)),
                  _const_spec((d, LANES))],
                 rs(d + LANES), jax.ShapeDtypeStruct((rows, d + LANES), F32),
                 (ya, yb, ym, x, w_out, g_ffn.reshape(1, d), w_router), name)


def _expert_ffn(n2, gates, expert_lane, wg_ref, wu_ref, wd_ref):
    hg = jnp.dot(n2, wg_ref[0], preferred_element_type=F32)
    hu = jnp.dot(n2, wu_ref[0], preferred_element_type=F32)
    lane = lax.broadcasted_iota(jnp.int32, gates.shape, 1)
    g = jnp.sum(jnp.where(lane == expert_lane, gates, 0.0), axis=-1, keepdims=True)
    act = hg * jax.nn.sigmoid(hg) * hu * g
    return jnp.dot(act.astype(BF16), wd_ref[0], preferred_element_type=F32)


def _moe_dense_body(rec_ref, gf_ref, wg_ref, wu_ref, wd_ref, o_ref, n2_ref):
    e = pl.program_id(1)
    d = o_ref.shape[1]

    @pl.when(e == 0)
    def _():
        h = rec_ref[:, :d]
        o_ref[...] = h
        n2_ref[...] = _row_norm(h, gf_ref)

    o_ref[...] += _expert_ffn(n2_ref[...], rec_ref[:, d:], e, wg_ref, wu_ref, wd_ref)


def _moe_dense(rec, g_ffn, w_gate, w_up, w_down, tm, name):
    rows = rec.shape[0]
    d, ff = w_gate.shape[1:]
    wspec = lambda a, b: pl.BlockSpec((1, a, b), lambda i, e: (e, 0, 0))
    return _call(_moe_dense_body, (rows // tm, N_EXPERTS),
                 [pl.BlockSpec((tm, d + LANES), lambda i, e: (i, 0)), _const_spec((1, d)),
                  wspec(d, ff), wspec(d, ff), wspec(ff, d)],
                 pl.BlockSpec((tm, d), lambda i, e: (i, 0)), jax.ShapeDtypeStruct((rows, d), F32),
                 (rec, g_ffn.reshape(1, d), w_gate, w_up, w_down), name,
                 semantics=("parallel", "arbitrary"), scratch=[pltpu.VMEM((tm, d), BF16)])


def _group_sort_plan(rec, tm):
    n, width = rec.shape
    d = width - LANES
    n_tiles = n // tm + N_GROUPS
    grp = rec[:, d + N_EXPERTS].astype(jnp.int32)
    order = jnp.argsort(grp, stable=True).astype(jnp.int32)
    counts = jnp.sum((grp[:, None] == jnp.arange(N_GROUPS)[None, :]).astype(jnp.int32), axis=0)
    tiles_per = (counts + tm - 1) // tm
    tile_end = jnp.cumsum(tiles_per)
    tile_start = tile_end - tiles_per
    tok_start = jnp.cumsum(counts) - counts
    n_used = tile_end[-1]
    t = jnp.arange(n_tiles, dtype=jnp.int32)
    tgrp = jnp.minimum(jnp.sum((t[:, None] >= tile_end[None, :]).astype(jnp.int32), axis=1), N_GROUPS - 1)
    row0 = (t - tile_start[tgrp]) * tm
    nvalid = jnp.where(t < n_used, jnp.clip(counts[tgrp] - row0, 0, tm), 0).astype(jnp.int32)
    k = row0[:, None] + jnp.arange(tm, dtype=jnp.int32)[None, :]
    real = jnp.arange(tm, dtype=jnp.int32)[None, :] < nvalid[:, None]
    src = jnp.where(real, order[jnp.clip(tok_start[tgrp][:, None] + k, 0, n - 1)], -1)
    return src.reshape(-1).astype(jnp.int32), tgrp.astype(jnp.int32), nvalid, n_used.reshape(1).astype(jnp.int32)


def _moe_grouped_body(src_ref, tgrp_ref, nvalid_ref, nused_ref, rec_hbm, gf_ref, wg_ref, wu_ref, wd_ref, y_hbm,
                      rec_buf, n2_ref, acc_ref, y_buf, gsem, ssem, *, tm):
    i = pl.program_id(0)
    e = pl.program_id(1)
    d = y_buf.shape[1]
    n_used = nused_ref[0]
    slot = i % 2

    def gather_rows(tile, slot_):
        def row(r, c):
            tok = jnp.maximum(src_ref[tile * tm + r], 0)
            pltpu.make_async_copy(rec_hbm.at[pl.ds(tok, 1)], rec_buf.at[slot_, pl.ds(r, 1)], gsem.at[slot_]).start()
            return c
        lax.fori_loop(0, tm, row, 0, unroll=8)

    def wait_gather(slot_):
        pltpu.make_async_copy(rec_hbm.at[pl.ds(0, tm)], rec_buf.at[slot_], gsem.at[slot_]).wait()

    def scatter_copy(tile, r):
        tok = src_ref[tile * tm + r]
        return pltpu.make_async_copy(y_buf.at[pl.ds(r, 1)], y_hbm.at[pl.ds(tok, 1)], ssem.at[0])

    def scatter_rows(tile, wait):
        def row(r, c):
            cp = scatter_copy(tile, r)
            if wait:
                cp.wait()
            else:
                cp.start()
            return c
        lax.fori_loop(0, nvalid_ref[tile], row, 0)

    @pl.when(i < n_used)
    def _():
        @pl.when(e == 0)
        def _():
            @pl.when(i == 0)
            def _():
                gather_rows(0, 0)

            wait_gather(slot)

            @pl.when(i + 1 < n_used)
            def _():
                gather_rows(i + 1, 1 - slot)

            n2_ref[...] = _row_norm(rec_buf[slot, :, :d], gf_ref)
            acc_ref[...] = jnp.zeros_like(acc_ref)

        lane0 = tgrp_ref[i] * EXPERTS_PER_GROUP
        acc_ref[...] += _expert_ffn(n2_ref[...], rec_buf[slot, :, d:], lane0 + e, wg_ref, wu_ref, wd_ref)

        @pl.when(e == EXPERTS_PER_GROUP - 1)
        def _():
            @pl.when(i > 0)
            def _():
                scatter_rows(i - 1, True)

            y_buf[...] = rec_buf[slot, :, :d] + acc_ref[...]
            scatter_rows(i, False)

            @pl.when(i == n_used - 1)
            def _():
                scatter_rows(i, True)


def _moe_grouped(rec, g_ffn, w_gate, w_up, w_down, tm, name):
    n = rec.shape[0]
    d, ff = w_gate.shape[1:]
    src, tgrp, nvalid, n_used = _group_sort_plan(rec, tm)
    n_tiles = tgrp.shape[0]

    def widx(i, e, src_ref, tgrp_ref, nvalid_ref, nused_ref):
        last = nused_ref[0] - 1
        live = i <= last
        return (tgrp_ref[jnp.minimum(i, last)] * EXPERTS_PER_GROUP
                + jnp.where(live, e, EXPERTS_PER_GROUP - 1), 0, 0)

    scratch = [pltpu.VMEM((2, tm, d + LANES), F32), pltpu.VMEM((tm, d), BF16), pltpu.VMEM((tm, d), F32),
               pltpu.VMEM((tm, d), F32), pltpu.SemaphoreType.DMA((2,)), pltpu.SemaphoreType.DMA((1,))]
    blk = 3 * _nbytes((d, ff), BF16) + _nbytes((1, d), F32)
    scr = sum(_nbytes(s.shape, s.dtype) for s in scratch[:4])
    return pl.pallas_call(
        functools.partial(_moe_grouped_body, tm=tm),
        grid_spec=pltpu.PrefetchScalarGridSpec(
            num_scalar_prefetch=4,
            grid=(n_tiles, EXPERTS_PER_GROUP),
            in_specs=[pl.BlockSpec(memory_space=pl.ANY),
                      pl.BlockSpec((1, d), lambda i, e, *_: (0, 0)),
                      pl.BlockSpec((1, d, ff), widx), pl.BlockSpec((1, d, ff), widx), pl.BlockSpec((1, ff, d), widx)],
            out_specs=pl.BlockSpec(memory_space=pl.ANY),
            scratch_shapes=scratch),
        out_shape=jax.ShapeDtypeStruct((n, d), F32),
        compiler_params=pltpu.CompilerParams(dimension_semantics=("arbitrary", "arbitrary"),
                                             vmem_limit_bytes=_vmem_limit(blk, scr)),
        name=name,
    )(src, tgrp, nvalid, n_used, rec, g_ffn.reshape(1, d), w_gate, w_up, w_down)


def kernel(x_prompt, x_sample, mem_prompt, cache_win_k, cache_win_v, cache_mem_k, cache_mem_v, rel_bias, g_mix, w_in, g_av, w_s, b_s, g_qb, g_kb, g_qm, g_km, g_mem, w_mem_kv, w_out, g_ffn, w_router_group, w_router_expert, w_gate, w_up, w_down):
    batch, seq, d = x_prompt.shape
    dec_batch, ds, _ = x_sample.shape
    depth = w_in.shape[0]
    n_cache = cache_win_k.shape[2]
    assert seq % (KEYS_BACK * DILATIONS[-1][1]) == 0 and seq <= DILATIONS[-1][0] and seq % CHUNK == 0
    assert n_cache >= DILATIONS[-1][0] and ds <= CHUNK

    xp = x_prompt.reshape(batch * seq, d)
    xs = x_sample.reshape(dec_batch * ds, d)
    mem = mem_prompt.reshape(batch * N_MEM, d)
    tm = 512
    rows_s = dec_batch * ds

    bias_prompt = _prompt_bias_tables(rel_bias)
    bias_sample = _sample_bias_tables(rel_bias, n_cache, ds)
    causal = np.tril(np.ones((CHUNK, CHUNK), bool))
    same_row = (np.arange(rows_s)[:, None] // ds) == (np.arange(rows_s)[None, :] // ds)

    outs = [[] for _ in range(8)]
    for l in range(depth):
        w_in_b = w_in[l].astype(BF16)
        w_out_b = w_out[l].astype(BF16)
        w_router = jnp.concatenate(
            [jnp.transpose(w_router_expert[l], (1, 0, 2)).reshape(d, N_EXPERTS), w_router_group[l],
             jnp.zeros((d, LANES - N_EXPERTS - N_GROUPS), F32)], axis=1).astype(BF16)
        wg_b, wu_b, wd_b = w_gate[l].astype(BF16), w_up[l].astype(BF16), w_down[l].astype(BF16)
        ws_prompt = jnp.where(causal, w_s[l], 0.0)
        bs_prompt = b_s[l].T
        ws_small = jnp.where(causal[:ds, :ds], w_s[l][:, :ds, :ds], 0.0)
        ws_sample = jnp.where(same_row, jnp.tile(ws_small, (1, dec_batch, dec_batch)), 0.0)
        bs_sample = jnp.tile(b_s[l][:, :ds].T, (dec_batch, 1))

        mk, mv = _memory_kv(mem, g_mem[l], w_mem_kv[l].astype(BF16), g_km[l])
        ya, va, qb, kb, vb, ym = _mixer_inputs(xp, g_mix[l], w_in_b, g_av[l], ws_prompt, bs_prompt, g_qb[l],
                                               g_kb[l], g_qm[l], (mk, mv, seq), tm, "mixer_inputs_prompt")
        yb = _dilated_attention_prompt(qb, kb, vb, bias_prompt, batch, seq)
        rec = _out_and_router(ya, yb, ym, xp, w_out_b, g_ffn[l], w_router, 256, "out_router_prompt")
        xp = _moe_grouped(rec, g_ffn[l], wg_b, wu_b, wd_b, tm, "moe_prompt")
        wp = min(DILATIONS[-1][0], seq)
        outs[0].append(kb.reshape(batch, seq, B_HEADS, B_HEAD_DIM)[:, seq - wp:])
        outs[1].append(vb.reshape(batch, seq, B_HEADS, B_HEAD_DIM)[:, seq - wp:])
        outs[2].append(va.reshape(batch, seq, A_WIDTH)[:, seq - CHUNK:].reshape(batch, CHUNK, A_GROUPS, A_DIM))
        outs[3].append(mk.reshape(batch, N_MEM, M_HEADS, M_HEAD_DIM))
        outs[4].append(mv.reshape(batch, N_MEM, M_HEADS, M_HEAD_DIM))

        ya, va, qb, kb, vb, qm = _mixer_inputs(xs, g_mix[l], w_in_b, g_av[l], ws_sample, bs_sample, g_qb[l],
                                               g_kb[l], g_qm[l], None, rows_s, "mixer_inputs_sample")
        yb, ym = _attention_sample(
            qb, kb, vb, cache_win_k[l], cache_win_v[l], bias_sample, qm, cache_mem_k[l], cache_mem_v[l], ds)
        rec = _out_and_router(ya, yb, ym, xs, w_out_b, g_ffn[l], w_router, rows_s, "out_router_sample")
        xs = _moe_dense(rec, g_ffn[l], wg_b, wu_b, wd_b, rows_s, "moe_sample")
        outs[5].append(kb.reshape(dec_batch, ds, B_HEADS, B_HEAD_DIM))
        outs[6].append(vb.reshape(dec_batch, ds, B_HEADS, B_HEAD_DIM))
        outs[7].append(va.reshape(dec_batch, ds, A_GROUPS, A_DIM))

    stacked = [jnp.stack(o, axis=0) for o in outs]
    return (xp.reshape(batch, seq, d), xs.reshape(dec_batch, ds, d), *stacked)
```

```python
import functools
import math

import numpy as np
import jax
import jax.numpy as jnp
from jax import lax
from jax.experimental import pallas as pl
from jax.experimental.pallas import tpu as pltpu

F32 = jnp.float32
BF16 = jnp.bfloat16

D_MODEL = 2048
A_GROUPS = 4
A_DIM = 128
A_WIDTH = A_GROUPS * A_DIM
CHUNK = 128
B_HEADS = 16
B_HEAD_DIM = 64
B_WIDTH = B_HEADS * B_HEAD_DIM
M_HEADS = 4
M_HEAD_DIM = 128
M_WIDTH = M_HEADS * M_HEAD_DIM
N_MEM = 256
DILATIONS = ((128, 1), (512, 4), (2048, 16))
N_DIL = len(DILATIONS)
KEYS_BACK = 128
RES = DILATIONS[1][1]
UNITS_IN_FLIGHT = 4
N_BUCKETS = 32
MAX_DISTANCE = 2048
N_GROUPS = 4
EXPERTS_PER_GROUP = 8
N_EXPERTS = N_GROUPS * EXPERTS_PER_GROUP
EPS = 1e-6

LANES = 128
V7X_VMEM_BYTES = 64 * 1024 * 1024
VMEM_CAP = V7X_VMEM_BYTES - 6 * 1024 * 1024
INV_SQRT2 = 0.7071067811865476
B_SCALE = B_HEAD_DIM ** -0.5
M_SCALE = M_HEAD_DIM ** -0.5
NEG_INF = float("-inf")

_NT = (((1,), (1,)), ((), ()))
_TN = (((0,), (0,)), ((), ()))


def _log2(n):
    assert n > 0 and n & (n - 1) == 0, n
    return n.bit_length() - 1


def _mm(a, b, dims=None):
    a, b = a.astype(BF16), b.astype(BF16)
    if dims is None:
        return jnp.dot(a, b, preferred_element_type=F32)
    return lax.dot_general(a, b, dims, preferred_element_type=F32)


def _vmem_limit(block_bytes, scratch_bytes=0):
    est = 2 * block_bytes + scratch_bytes + 16 * 1024 * 1024
    return int(min(max(est, 32 * 1024 * 1024), VMEM_CAP))


def _nbytes(shape, dtype):
    return int(np.prod(shape)) * jnp.dtype(dtype).itemsize


def _call(body, grid, in_specs, out_specs, out_shape, operands, name, semantics=None, scratch=()):
    outs = out_shape if isinstance(out_shape, (list, tuple)) else [out_shape]
    ospecs = out_specs if isinstance(out_specs, (list, tuple)) else [out_specs]
    blk = sum(_nbytes(s.block_shape, o.dtype) for s, o in zip(in_specs, operands))
    blk += sum(_nbytes(s.block_shape, o.dtype) for s, o in zip(ospecs, outs))
    scr = sum(_nbytes(s.shape, s.dtype) for s in scratch)
    if semantics is None:
        semantics = ("parallel",) * len(grid)
    return pl.pallas_call(
        body,
        grid=grid,
        in_specs=in_specs,
        out_specs=out_specs,
        out_shape=out_shape,
        scratch_shapes=list(scratch),
        compiler_params=pltpu.CompilerParams(dimension_semantics=semantics,
                                             vmem_limit_bytes=_vmem_limit(blk, scr)),
        name=name,
    )(*operands)


def _const_spec(shape):
    nd = len(shape)
    return pl.BlockSpec(shape, lambda *_: (0,) * nd, pipeline_mode=pl.Buffered(1))


def _row_norm(x, g_ref):
    return (x * lax.rsqrt(jnp.mean(x * x, axis=-1, keepdims=True) + EPS) * g_ref[...]).astype(BF16)


def _store_head_norm(z, g_ref, o_ref, head_dim):
    lane = lax.broadcasted_iota(jnp.int32, (z.shape[0], LANES), 1)
    for c0 in range(0, z.shape[1], LANES):
        blk = z[:, c0:c0 + LANES]
        sq = blk * blk
        if head_dim == LANES:
            inv = lax.rsqrt(jnp.mean(sq, axis=-1, keepdims=True) + EPS)
        else:
            lo = lane < head_dim
            s_lo = jnp.sum(jnp.where(lo, sq, 0.0), axis=-1, keepdims=True)
            s_hi = jnp.sum(jnp.where(lo, 0.0, sq), axis=-1, keepdims=True)
            inv = jnp.where(lo, lax.rsqrt(s_lo / head_dim + EPS), lax.rsqrt(s_hi / head_dim + EPS))
        o_ref[:, c0:c0 + LANES] = blk * inv * g_ref[:, c0:c0 + LANES]


def _memory_attend_heads(qm, mk_ref, mv_ref, ym_ref):
    for h in range(M_HEADS):
        c0 = h * M_HEAD_DIM
        if len(mk_ref.shape) == 3:
            mk, mv = mk_ref[:, h, :], mv_ref[:, h, :]
        else:
            mk, mv = mk_ref[:, c0:c0 + M_HEAD_DIM], mv_ref[:, c0:c0 + M_HEAD_DIM]
        s = _mm(qm[:, c0:c0 + M_HEAD_DIM], mk, _NT) * M_SCALE
        p = jnp.exp(s - jnp.max(s, axis=-1, keepdims=True))
        p = p / jnp.sum(p, axis=-1, keepdims=True)
        ym_ref[:, c0:c0 + M_HEAD_DIM] = _mm(p, mv)


def _mixer_inputs_body(*refs, cs, attend_memory):
    (x_ref, gmix_ref, w_ref, gav_ref, ws_ref, bs_ref, gq_ref, gk_ref, gqm_ref), refs = refs[:9], refs[9:]
    if attend_memory:
        (mk_ref, mv_ref), refs = refs[:2], refs[2:]
    ya_ref, va_ref, q_ref, k_ref, v_ref, m_ref, n_ref = refs[:7]
    o_b = 2 * A_WIDTH
    o_k = o_b + B_WIDTH
    o_v = o_k + B_WIDTH
    o_m = o_v + B_WIDTH
    n_ref[...] = _row_norm(x_ref[...], gmix_ref)
    tm = n_ref.shape[0]

    z = _mm(n_ref[...], w_ref[:, :o_b])
    uv = 0.5 * z * (1.0 + lax.erf(z * INV_SQRT2))
    for g in range(A_GROUPS):
        c0 = g * A_DIM
        u = uv[:, c0:c0 + A_DIM]
        v = uv[:, A_WIDTH + c0:A_WIDTH + c0 + A_DIM]
        va = v * lax.rsqrt(jnp.mean(v * v, axis=-1, keepdims=True) + EPS) * gav_ref[:, c0:c0 + A_DIM]
        va_ref[:, c0:c0 + A_DIM] = va
        for r0 in range(0, tm, cs):
            mixed = _mm(ws_ref[g], va[r0:r0 + cs]) + bs_ref[:, g:g + 1]
            ya_ref[r0:r0 + cs, c0:c0 + A_DIM] = u[r0:r0 + cs] * mixed

    _store_head_norm(_mm(n_ref[...], w_ref[:, o_b:o_k]), gq_ref, q_ref, B_HEAD_DIM)
    _store_head_norm(_mm(n_ref[...], w_ref[:, o_k:o_v]), gk_ref, k_ref, B_HEAD_DIM)
    v_ref[...] = _mm(n_ref[...], w_ref[:, o_v:o_m])
    if attend_memory:
        qm_ref = refs[7]
        _store_head_norm(_mm(n_ref[...], w_ref[:, o_m:]), gqm_ref, qm_ref, M_HEAD_DIM)
        _memory_attend_heads(qm_ref[...], mk_ref, mv_ref, m_ref)
    else:
        _store_head_norm(_mm(n_ref[...], w_ref[:, o_m:]), gqm_ref, m_ref, M_HEAD_DIM)


def _mixer_inputs(x, g_mix, w_in, g_av, ws_masked, bs_col, g_qb, g_kb, g_qm, memory, tm, name):
    rows, d = x.shape
    cs = ws_masked.shape[1]

    def tiled(gain, heads):
        return jnp.tile(gain.reshape(1, -1), (1, heads))

    def rs(width):
        return pl.BlockSpec((tm, width), lambda i: (i, 0))

    in_specs = [rs(d), _const_spec((1, d)), _const_spec(w_in.shape), _const_spec((1, A_WIDTH)),
                _const_spec((A_GROUPS, cs, cs)), _const_spec((cs, A_GROUPS)), _const_spec((1, B_WIDTH)),
                _const_spec((1, B_WIDTH)), _const_spec((1, M_WIDTH))]
    operands = [x, g_mix.reshape(1, d), w_in, g_av.reshape(1, A_WIDTH), ws_masked, bs_col,
                tiled(g_qb, B_HEADS), tiled(g_kb, B_HEADS), tiled(g_qm, M_HEADS)]
    scratch = [pltpu.VMEM((tm, d), BF16)]
    if memory is not None:
        mk, mv, seq = memory
        mem = pl.BlockSpec((N_MEM, M_WIDTH), lambda i: (i // (seq // tm), 0))
        in_specs += [mem, mem]
        operands += [mk, mv]
        scratch.append(pltpu.VMEM((tm, M_WIDTH), F32))
    widths = (A_WIDTH, A_WIDTH, B_WIDTH, B_WIDTH, B_WIDTH, M_WIDTH)
    return _call(functools.partial(_mixer_inputs_body, cs=cs, attend_memory=memory is not None), (rows // tm,),
                 in_specs, [rs(w) for w in widths], [jax.ShapeDtypeStruct((rows, w), F32) for w in widths],
                 operands, name, scratch=scratch)


def _mem_kv_body(x_ref, gmem_ref, w_ref, g_ref, mk_ref, mv_ref):
    kv = _mm(_row_norm(x_ref[...], gmem_ref), w_ref[...])
    _store_head_norm(kv[:, :M_WIDTH], g_ref, mk_ref, M_HEAD_DIM)
    mv_ref[...] = kv[:, M_WIDTH:]


def _memory_kv(mem, g_mem, w_kv, g_km):
    rows, d = mem.shape
    out = jax.ShapeDtypeStruct((rows, M_WIDTH), F32)
    spec = pl.BlockSpec((N_MEM, M_WIDTH), lambda i: (i, 0))
    return _call(_mem_kv_body, (rows // N_MEM,),
                 [pl.BlockSpec((N_MEM, d), lambda i: (i, 0)), _const_spec((1, d)), _const_spec((d, 2 * M_WIDTH)),
                  _const_spec((1, M_WIDTH))],
                 [spec, spec], [out, out],
                 (mem, g_mem.reshape(1, d), w_kv, jnp.tile(g_km.reshape(1, M_HEAD_DIM), (1, M_HEADS))), "memory_kv")


def _t5_bucket_np(dist):
    max_exact = N_BUCKETS // 2
    d = np.maximum(dist, 0)
    df = np.maximum(d, 1).astype(np.float64)
    large = max_exact + (np.log(df / max_exact) / math.log(MAX_DISTANCE / max_exact)
                         * (N_BUCKETS - max_exact)).astype(np.int32)
    large = np.minimum(large, N_BUCKETS - 1)
    return np.where(d < max_exact, d, large)


def _bias_by_distance(rel_bias, dists):
    onehot = np.zeros((N_BUCKETS, len(dists)), np.float32)
    onehot[_t5_bucket_np(np.asarray(dists)), np.arange(len(dists))] = 1.0
    return jnp.dot(rel_bias.astype(F32).T, jnp.asarray(onehot), precision=lax.Precision.HIGHEST)


def _prompt_bias_tables(rel_bias):
    wrap = 2 * KEYS_BACK + 1
    tabs = []
    for _, dil in DILATIONS:
        by_t = _bias_by_distance(rel_bias, [(KEYS_BACK - t) * dil for t in range(KEYS_BACK + 1)])
        row = jnp.concatenate([by_t, jnp.full((B_HEADS, wrap - KEYS_BACK - 1), NEG_INF, F32)], axis=1)
        flat = jnp.tile(row, (1, KEYS_BACK))[:, :KEYS_BACK * 2 * KEYS_BACK]
        tabs.append(flat.reshape(B_HEADS, KEYS_BACK, 2 * KEYS_BACK))
    return jnp.stack(tabs, axis=0)


def _sample_bias_tables(rel_bias, n_cache, ds):
    n_keys = n_cache + ds
    dist = n_cache + np.arange(ds)[:, None] - np.arange(n_keys)[None, :]
    by_key = _bias_by_distance(rel_bias, [max(n_cache + ds - 1 - c, 0) for c in range(n_keys + ds - 1)])
    bias = jnp.stack([by_key[:, ds - 1 - i:ds - 1 - i + n_keys] for i in range(ds)], axis=1)
    bias = bias.reshape(B_HEADS * ds, n_keys)
    tabs = []
    for window, dil in DILATIONS:
        used = (dist >= 0) & (dist % dil == 0) & (dist <= window)
        used = np.broadcast_to(used[None], (B_HEADS, ds, n_keys)).reshape(B_HEADS * ds, n_keys)
        tabs.append(jnp.where(used, bias, NEG_INF))
    return jnp.stack(tabs, axis=0)


def _dilated_prompt_body(q_ref, k_ref, v_ref, b_ref, o_ref, qd_ref, kd_ref, vd_ref, od_ref, lse_ref, *, seq):
    per = seq // RES
    lane = lax.broadcasted_iota(jnp.int32, (KEYS_BACK, LANES), 1)
    lo = lane < B_HEAD_DIM

    for c in range(RES):
        qd_ref[pl.ds(c * per, per), :] = q_ref[pl.ds(c, per, stride=RES), :] * B_SCALE
        kd_ref[pl.ds(c * per, per), :] = k_ref[pl.ds(c, per, stride=RES), :]
        vd_ref[pl.ds(c * per, per), :] = v_ref[pl.ds(c, per, stride=RES), :]

    def attend(units):
        nk = units[0][2].shape[0]
        off = 2 * KEYS_BACK - nk
        scores, biases = [], []
        for di, qv, kv, _, _ in units:
            kb = kv.astype(BF16)
            for h in range(2):
                qh = jnp.where(lo if h == 0 else jnp.logical_not(lo), qv, 0.0).astype(BF16)
                scores.append(lax.dot_general(qh, kb, _NT, preferred_element_type=F32))
                biases.append(b_ref[di, h, :, off:])
        s = jnp.concatenate(scores, axis=0) + jnp.concatenate(biases, axis=0)
        m = jnp.max(s, axis=-1, keepdims=True)
        p = jnp.exp(s - m)
        l = jnp.sum(p, axis=-1, keepdims=True)
        pn = (p * (1.0 / l)).astype(BF16)
        lse = m + jnp.log(l)
        for g, (di, _, _, vv, out_rows) in enumerate(units):
            vb = vv.astype(BF16)
            r0 = 2 * g * KEYS_BACK
            r1 = r0 + KEYS_BACK
            o0 = jnp.dot(pn[r0:r1], vb, preferred_element_type=F32)
            o1 = jnp.dot(pn[r1:r1 + KEYS_BACK], vb, preferred_element_type=F32)
            od_ref[di, out_rows, :] = jnp.where(lo, o0, o1)
            lse_ref[di, out_rows, :] = jnp.where(lo, lse[r0:r1], lse[r1:r1 + KEYS_BACK])

    def loop(n, group, make_unit):
        assert n % group == 0

        def step(t, carry):
            attend([make_unit(t * group + g) for g in range(group)])
            return carry
        lax.fori_loop(0, n // group, step, 0)

    blocks = seq // KEYS_BACK
    sub_blocks = per // KEYS_BACK

    def dense_unit(start):
        rows = pl.ds(start, KEYS_BACK)
        keys = rows if isinstance(start, int) and start == 0 else pl.ds(start - KEYS_BACK, 2 * KEYS_BACK)
        return 0, q_ref[rows, :] * B_SCALE, k_ref[keys, :], v_ref[keys, :], rows

    def mid_unit(start, first):
        rows = pl.ds(start, KEYS_BACK)
        keys = rows if first else pl.ds(start - KEYS_BACK, 2 * KEYS_BACK)
        return 1, qd_ref[rows, :], kd_ref[keys, :], vd_ref[keys, :], rows

    def wide_unit(r):
        rows = pl.ds((r & (RES - 1)) * per + (r >> _log2(RES)), KEYS_BACK, stride=RES)
        return 2, qd_ref[rows, :], kd_ref[rows, :], vd_ref[rows, :], rows

    attend([dense_unit(0)] + [mid_unit(c * per, True) for c in range(RES)])
    loop(blocks - 1, max(f for f in range(1, UNITS_IN_FLIGHT + 1) if (blocks - 1) % f == 0),
         lambda u: dense_unit(pl.multiple_of((u + 1) * KEYS_BACK, KEYS_BACK)))
    loop(RES * (sub_blocks - 1), RES,
         lambda u: mid_unit(pl.multiple_of((u & (RES - 1)) * per + (1 + (u >> _log2(RES))) * KEYS_BACK, KEYS_BACK),
                            False))
    loop(RES * RES, RES, wide_unit)

    for c in range(RES):
        for j in range(sub_blocks):
            nat = pl.ds(c + RES * j * KEYS_BACK, KEYS_BACK, stride=RES)
            grp = pl.ds(c * per + j * KEYS_BACK, KEYS_BACK)
            lses = [lse_ref[0, nat, :], lse_ref[1, grp, :], lse_ref[2, grp, :]]
            outs = [od_ref[0, nat, :], od_ref[1, grp, :], od_ref[2, grp, :]]
            top = jnp.maximum(jnp.maximum(lses[0], lses[1]), lses[2])
            ws = [jnp.exp(x - top) for x in lses]
            den = ws[0] + ws[1] + ws[2]
            o_ref[nat, :] = (ws[0] / den) * outs[0] + (ws[1] / den) * outs[1] + (ws[2] / den) * outs[2]


def _dilated_attention_prompt(q, k, v, bias_tabs, batch, seq):
    assert DILATIONS == ((KEYS_BACK, 1), (KEYS_BACK * RES, RES), (KEYS_BACK * RES * RES, RES * RES))
    assert seq == KEYS_BACK * RES * RES
    pairs = B_WIDTH // LANES
    blk = pl.BlockSpec((seq, LANES), lambda b, p: (b, p))
    return _call(functools.partial(_dilated_prompt_body, seq=seq), (batch, pairs),
                 [blk, blk, blk, pl.BlockSpec((N_DIL, 2, KEYS_BACK, 2 * KEYS_BACK), lambda b, p: (0, p, 0, 0))],
                 blk, jax.ShapeDtypeStruct((batch * seq, B_WIDTH), F32), (q, k, v, bias_tabs),
                 "dilated_attention_prompt",
                 scratch=[pltpu.VMEM((seq, LANES), F32)] * 3 + [pltpu.VMEM((N_DIL, seq, LANES), F32)] * 2)


def _sample_attn_body(q_ref, kn_ref, vn_ref, kt_ref, vt_ref, tf_ref, tc_ref, tn_ref, qm_ref, mk_ref, mv_ref,
                      yb_ref, ym_ref, *, ds, far):
    rows = B_HEADS * ds
    row = lax.broadcasted_iota(jnp.int32, (rows, B_WIDTH), 0)
    col = lax.broadcasted_iota(jnp.int32, (rows, B_WIDTH), 1)
    own = (row >> _log2(ds)) == (col >> _log2(B_HEAD_DIM))
    q = q_ref[...] * B_SCALE
    qbd = jnp.where(own, jnp.concatenate([q] * B_HEADS, axis=0), 0.0).astype(BF16)
    s_f = _mm(qbd, kt_ref[:, :far]) + tf_ref[...]
    s_c = _mm(qbd, kt_ref[:, far:])
    s_n = _mm(qbd, kn_ref[...], _NT)
    pc, pn, lses = [], [], []
    p_far = None
    for di in range(N_DIL):
        widest = di == N_DIL - 1
        sc = s_c + tc_ref[di]
        sn = s_n + tn_ref[di]
        m = jnp.maximum(jnp.max(sc, axis=-1, keepdims=True), jnp.max(sn, axis=-1, keepdims=True))
        if widest:
            m = jnp.maximum(m, jnp.max(s_f, axis=-1, keepdims=True))
        ec = jnp.exp(sc - m)
        en = jnp.exp(sn - m)
        l = jnp.sum(ec, axis=-1, keepdims=True) + jnp.sum(en, axis=-1, keepdims=True)
        if widest:
            ef = jnp.exp(s_f - m)
            l = l + jnp.sum(ef, axis=-1, keepdims=True)
        inv = 1.0 / l
        if widest:
            p_far = (ef * inv).astype(BF16)
        pc.append((ec * inv).astype(BF16))
        pn.append((en * inv).astype(BF16))
        lses.append(m + jnp.log(l))
    o = (_mm(jnp.concatenate(pc, axis=0), vt_ref[:, far:], _NT)
         + _mm(jnp.concatenate(pn, axis=0), vn_ref[...]))
    o_far = _mm(p_far, vt_ref[:, :far], _NT)
    top = jnp.maximum(jnp.maximum(lses[0], lses[1]), lses[2])
    ws = [jnp.exp(x - top) for x in lses]
    den = ws[0] + ws[1] + ws[2]
    mixed = (ws[N_DIL - 1] / den) * (o[(N_DIL - 1) * rows:] + o_far)
    for di in range(N_DIL - 1):
        mixed = mixed + (ws[di] / den) * o[di * rows:(di + 1) * rows]
    mixed = jnp.where(own, mixed, 0.0)
    yb = mixed[0:ds]
    for h in range(1, B_HEADS):
        yb = yb + mixed[h * ds:(h + 1) * ds]
    yb_ref[...] = yb
    _memory_attend_heads(qm_ref[...], mk_ref, mv_ref, ym_ref)


def _attention_sample(q, k_new, v_new, cache_k, cache_v, tabs, qm, cache_mk, cache_mv, ds):
    batch, n_cache = cache_k.shape[:2]
    rows = B_HEADS * ds
    far = n_cache - DILATIONS[-2][0]
    assert N_DIL == 3 and rows == LANES and far % LANES == 0
    assert all(w0 <= w1 for (w0, _), (w1, _) in zip(DILATIONS, DILATIONS[1:]))

    def keys_minor(x):
        return jnp.transpose(x, (0, 2, 3, 1)).reshape(batch * B_WIDTH, n_cache)

    new = pl.BlockSpec((ds, B_WIDTH), lambda b: (b, 0))
    cache = pl.BlockSpec((B_WIDTH, n_cache), lambda b: (b, 0))
    mem = pl.BlockSpec((N_MEM, M_HEADS, M_HEAD_DIM), lambda b: (b, 0, 0))
    cache_mk = cache_mk.reshape(batch * N_MEM, M_HEADS, M_HEAD_DIM)
    cache_mv = cache_mv.reshape(batch * N_MEM, M_HEADS, M_HEAD_DIM)
    qm_spec = pl.BlockSpec((ds, M_WIDTH), lambda b: (b, 0))
    return _call(functools.partial(_sample_attn_body, ds=ds, far=far), (batch,),
                 [new, new, new, cache, cache, _const_spec((rows, far)), _const_spec((N_DIL, rows, n_cache - far)),
                  _const_spec((N_DIL, rows, ds)), qm_spec, mem, mem],
                 [new, qm_spec],
                 [jax.ShapeDtypeStruct((batch * ds, B_WIDTH), F32), jax.ShapeDtypeStruct((batch * ds, M_WIDTH), F32)],
                 (q, k_new, v_new, keys_minor(cache_k), keys_minor(cache_v), tabs[N_DIL - 1, :, :far],
                  tabs[:, :, far:n_cache], tabs[:, :, n_cache:], qm, cache_mk, cache_mv),
                 "attention_sample")


def _out_router_body(ya_ref, yb_ref, ym_ref, x_ref, wo_ref, gf_ref, wr_ref, rec_ref):
    o_b = A_WIDTH
    o_m = A_WIDTH + B_WIDTH
    d = x_ref.shape[1]
    mix = (_mm(ya_ref[...], wo_ref[0:o_b, :]) + _mm(yb_ref[...], wo_ref[o_b:o_m, :])
           + _mm(ym_ref[...], wo_ref[o_m:, :]))
    h = x_ref[...] + mix
    rec_ref[:, :d] = h
    logits = _mm(_row_norm(h, gf_ref), wr_ref[...])
    lane_i = lax.broadcasted_iota(jnp.int32, logits.shape, 1)
    lane = lane_i.astype(F32)
    big = float(LANES)

    def first_argmax(vals):
        top = jnp.max(vals, axis=-1, keepdims=True)
        return top, jnp.min(jnp.where(vals == top, lane, big), axis=-1, keepdims=True)

    is_group = (lane_i >= N_EXPERTS) & (lane_i < N_EXPERTS + N_GROUPS)
    _, g_lane = first_argmax(jnp.where(is_group, logits, NEG_INF))
    grp = g_lane.astype(jnp.int32) - N_EXPERTS
    in_grp = (lane_i < N_EXPERTS) & ((lane_i >> _log2(EXPERTS_PER_GROUP)) == grp)
    e_log = jnp.where(in_grp, logits, NEG_INF)
    v1, i1 = first_argmax(e_log)
    v2, i2 = first_argmax(jnp.where(lane == i1, NEG_INF, e_log))
    e2 = jnp.exp(v2 - v1)
    den = 1.0 + e2
    rec_ref[:, d:] = (jnp.where(lane == i1, 1.0 / den, 0.0) + jnp.where(lane == i2, e2 / den, 0.0)
                      + jnp.where(lane_i == N_EXPERTS, grp.astype(F32), 0.0))


def _out_and_router(ya, yb, ym, x, w_out, g_ffn, w_router, tm, name):
    rows, d = x.shape

    def rs(width):
        return pl.BlockSpec((tm, width), lambda i: (i, 0))

    return _call(_out_router_body, (rows // tm,),
                 [rs(A_WIDTH), rs(B_WIDTH), rs(M_WIDTH), rs(d), _const_spec((d, d)), _const_spec((1, d)),
                  _const_spec((d, LANES))],
                 rs(d + LANES), jax.ShapeDtypeStruct((rows, d + LANES), F32),
                 (ya, yb, ym, x, w_out, g_ffn.reshape(1, d), w_router), name)


def _expert_ffn(n2, gates, expert_lane, wg_ref, wu_ref, wd_ref):
    hg = jnp.dot(n2, wg_ref[0], preferred_element_type=F32)
    hu = jnp.dot(n2, wu_ref[0], preferred_element_type=F32)
    lane = lax.broadcasted_iota(jnp.int32, gates.shape, 1)
    g = jnp.sum(jnp.where(lane == expert_lane, gates, 0.0), axis=-1, keepdims=True)
    act = hg * jax.nn.sigmoid(hg) * hu * g
    return jnp.dot(act.astype(BF16), wd_ref[0], preferred_element_type=F32)


def _moe_dense_body(rec_ref, gf_ref, wg_ref, wu_ref, wd_ref, o_ref, n2_ref):
    e = pl.program_id(1)
    d = o_ref.shape[1]

    @pl.when(e == 0)
    def _():
        h = rec_ref[:, :d]
        o_ref[...] = h
        n2_ref[...] = _row_norm(h, gf_ref)

    o_ref[...] += _expert_ffn(n2_ref[...], rec_ref[:, d:], e, wg_ref, wu_ref, wd_ref)


def _moe_dense(rec, g_ffn, w_gate, w_up, w_down, tm, name):
    rows = rec.shape[0]
    d, ff = w_gate.shape[1:]
    wspec = lambda a, b: pl.BlockSpec((1, a, b), lambda i, e: (e, 0, 0))
    return _call(_moe_dense_body, (rows // tm, N_EXPERTS),
                 [pl.BlockSpec((tm, d + LANES), lambda i, e: (i, 0)), _const_spec((1, d)),
                  wspec(d, ff), wspec(d, ff), wspec(ff, d)],
                 pl.BlockSpec((tm, d), lambda i, e: (i, 0)), jax.ShapeDtypeStruct((rows, d), F32),
                 (rec, g_ffn.reshape(1, d), w_gate, w_up, w_down), name,
                 semantics=("parallel", "arbitrary"), scratch=[pltpu.VMEM((tm, d), BF16)])


def _group_sort_plan(rec, tm):
    n, width = rec.shape
    d = width - LANES
    n_tiles = n // tm + N_GROUPS
    grp = rec[:, d + N_EXPERTS].astype(jnp.int32)
    order = jnp.argsort(grp, stable=True).astype(jnp.int32)
    counts = jnp.sum((grp[:, None] == jnp.arange(N_GROUPS)[None, :]).astype(jnp.int32), axis=0)
    tiles_per = (counts + tm - 1) // tm
    tile_end = jnp.cumsum(tiles_per)
    tile_start = tile_end - tiles_per
    tok_start = jnp.cumsum(counts) - counts
    n_used = tile_end[-1]
    t = jnp.arange(n_tiles, dtype=jnp.int32)
    tgrp = jnp.minimum(jnp.sum((t[:, None] >= tile_end[None, :]).astype(jnp.int32), axis=1), N_GROUPS - 1)
    row0 = (t - tile_start[tgrp]) * tm
    nvalid = jnp.where(t < n_used, jnp.clip(counts[tgrp] - row0, 0, tm), 0).astype(jnp.int32)
    k = row0[:, None] + jnp.arange(tm, dtype=jnp.int32)[None, :]
    real = jnp.arange(tm, dtype=jnp.int32)[None, :] < nvalid[:, None]
    src = jnp.where(real, order[jnp.clip(tok_start[tgrp][:, None] + k, 0, n - 1)], -1)
    return src.reshape(-1).astype(jnp.int32), tgrp.astype(jnp.int32), nvalid, n_used.reshape(1).astype(jnp.int32)


def _moe_grouped_body(src_ref, tgrp_ref, nvalid_ref, nused_ref, rec_hbm, gf_ref, wg_ref, wu_ref, wd_ref, y_hbm,
                      rec_buf, n2_ref, acc_ref, y_buf, gsem, ssem, *, tm):
    i = pl.program_id(0)
    e = pl.program_id(1)
    d = y_buf.shape[1]
    n_used = nused_ref[0]
    slot = i % 2

    def gather_rows(tile, slot_):
        def row(r, c):
            tok = jnp.maximum(src_ref[tile * tm + r], 0)
            pltpu.make_async_copy(rec_hbm.at[pl.ds(tok, 1)], rec_buf.at[slot_, pl.ds(r, 1)], gsem.at[slot_]).start()
            return c
        lax.fori_loop(0, tm, row, 0, unroll=8)

    def wait_gather(slot_):
        pltpu.make_async_copy(rec_hbm.at[pl.ds(0, tm)], rec_buf.at[slot_], gsem.at[slot_]).wait()

    def scatter_copy(tile, r):
        tok = src_ref[tile * tm + r]
        return pltpu.make_async_copy(y_buf.at[pl.ds(r, 1)], y_hbm.at[pl.ds(tok, 1)], ssem.at[0])

    def scatter_rows(tile, wait):
        def row(r, c):
            cp = scatter_copy(tile, r)
            if wait:
                cp.wait()
            else:
                cp.start()
            return c
        lax.fori_loop(0, nvalid_ref[tile], row, 0)

    @pl.when(i < n_used)
    def _():
        @pl.when(e == 0)
        def _():
            @pl.when(i == 0)
            def _():
                gather_rows(0, 0)

            wait_gather(slot)

            @pl.when(i + 1 < n_used)
            def _():
                gather_rows(i + 1, 1 - slot)

            n2_ref[...] = _row_norm(rec_buf[slot, :, :d], gf_ref)
            acc_ref[...] = jnp.zeros_like(acc_ref)

        lane0 = tgrp_ref[i] * EXPERTS_PER_GROUP
        acc_ref[...] += _expert_ffn(n2_ref[...], rec_buf[slot, :, d:], lane0 + e, wg_ref, wu_ref, wd_ref)

        @pl.when(e == EXPERTS_PER_GROUP - 1)
        def _():
            @pl.when(i > 0)
            def _():
                scatter_rows(i - 1, True)

            y_buf[...] = rec_buf[slot, :, :d] + acc_ref[...]
            scatter_rows(i, False)

            @pl.when(i == n_used - 1)
            def _():
                scatter_rows(i, True)


def _moe_grouped(rec, g_ffn, w_gate, w_up, w_down, tm, name):
    n = rec.shape[0]
    d, ff = w_gate.shape[1:]
    src, tgrp, nvalid, n_used = _group_sort_plan(rec, tm)
    n_tiles = tgrp.shape[0]

    def widx(i, e, src_ref, tgrp_ref, nvalid_ref, nused_ref):
        last = nused_ref[0] - 1
        live = i <= last
        return (tgrp_ref[jnp.minimum(i, last)] * EXPERTS_PER_GROUP
                + jnp.where(live, e, EXPERTS_PER_GROUP - 1), 0, 0)

    scratch = [pltpu.VMEM((2, tm, d + LANES), F32), pltpu.VMEM((tm, d), BF16), pltpu.VMEM((tm, d), F32),
               pltpu.VMEM((tm, d), F32), pltpu.SemaphoreType.DMA((2,)), pltpu.SemaphoreType.DMA((1,))]
    blk = 3 * _nbytes((d, ff), BF16) + _nbytes((1, d), F32)
    scr = sum(_nbytes(s.shape, s.dtype) for s in scratch[:4])
    return pl.pallas_call(
        functools.partial(_moe_grouped_body, tm=tm),
        grid_spec=pltpu.PrefetchScalarGridSpec(
            num_scalar_prefetch=4,
            grid=(n_tiles, EXPERTS_PER_GROUP),
            in_specs=[pl.BlockSpec(memory_space=pl.ANY),
                      pl.BlockSpec((1, d), lambda i, e, *_: (0, 0)),
                      pl.BlockSpec((1, d, ff), widx), pl.BlockSpec((1, d, ff), widx), pl.BlockSpec((1, ff, d), widx)],
            out_specs=pl.BlockSpec(memory_space=pl.ANY),
            scratch_shapes=scratch),
        out_shape=jax.ShapeDtypeStruct((n, d), F32),
        compiler_params=pltpu.CompilerParams(dimension_semantics=("arbitrary", "arbitrary"),
                                             vmem_limit_bytes=_vmem_limit(blk, scr)),
        name=name,
    )(src, tgrp, nvalid, n_used, rec, g_ffn.reshape(1, d), w_gate, w_up, w_down)


def kernel(x_prompt, x_sample, mem_prompt, cache_win_k, cache_win_v, cache_mem_k, cache_mem_v, rel_bias, g_mix, w_in, g_av, w_s, b_s, g_qb, g_kb, g_qm, g_km, g_mem, w_mem_kv, w_out, g_ffn, w_router_group, w_router_expert, w_gate, w_up, w_down):
    batch, seq, d = x_prompt.shape
    dec_batch, ds, _ = x_sample.shape
    depth = w_in.shape[0]
    n_cache = cache_win_k.shape[2]
    assert seq % (KEYS_BACK * DILATIONS[-1][1]) == 0 and seq <= DILATIONS[-1][0] and seq % CHUNK == 0
    assert n_cache >= DILATIONS[-1][0] and ds <= CHUNK

    xp = x_prompt.reshape(batch * seq, d)
    xs = x_sample.reshape(dec_batch * ds, d)
    mem = mem_prompt.reshape(batch * N_MEM, d)
    tm = 512
    rows_s = dec_batch * ds

    bias_prompt = _prompt_bias_tables(rel_bias)
    bias_sample = _sample_bias_tables(rel_bias, n_cache, ds)
    causal = np.tril(np.ones((CHUNK, CHUNK), bool))
    same_row = (np.arange(rows_s)[:, None] // ds) == (np.arange(rows_s)[None, :] // ds)

    outs = [[] for _ in range(8)]
    for l in range(depth):
        w_in_b = w_in[l].astype(BF16)
        w_out_b = w_out[l].astype(BF16)
        w_router = jnp.concatenate(
            [jnp.transpose(w_router_expert[l], (1, 0, 2)).reshape(d, N_EXPERTS), w_router_group[l],
             jnp.zeros((d, LANES - N_EXPERTS - N_GROUPS), F32)], axis=1).astype(BF16)
        wg_b, wu_b, wd_b = w_gate[l].astype(BF16), w_up[l].astype(BF16), w_down[l].astype(BF16)
        ws_prompt = jnp.where(causal, w_s[l], 0.0)
        bs_prompt = b_s[l].T
        ws_small = jnp.where(causal[:ds, :ds], w_s[l][:, :ds, :ds], 0.0)
        ws_sample = jnp.where(same_row, jnp.tile(ws_small, (1, dec_batch, dec_batch)), 0.0)
        bs_sample = jnp.tile(b_s[l][:, :ds].T, (dec_batch, 1))

        mk, mv = _memory_kv(mem, g_mem[l], w_mem_kv[l].astype(BF16), g_km[l])
        ya, va, qb, kb, vb, ym = _mixer_inputs(xp, g_mix[l], w_in_b, g_av[l], ws_prompt, bs_prompt, g_qb[l],
                                               g_kb[l], g_qm[l], (mk, mv, seq), tm, "mixer_inputs_prompt")
        yb = _dilated_attention_prompt(qb, kb, vb, bias_prompt, batch, seq)
        rec = _out_and_router(ya, yb, ym, xp, w_out_b, g_ffn[l], w_router, 256, "out_router_prompt")
        xp = _moe_grouped(rec, g_ffn[l], wg_b, wu_b, wd_b, tm, "moe_prompt")
        wp = min(DILATIONS[-1][0], seq)
        outs[0].append(kb.reshape(batch, seq, B_HEADS, B_HEAD_DIM)[:, seq - wp:])
        outs[1].append(vb.reshape(batch, seq, B_HEADS, B_HEAD_DIM)[:, seq - wp:])
        outs[2].append(va.reshape(batch, seq, A_WIDTH)[:, seq - CHUNK:].reshape(batch, CHUNK, A_GROUPS, A_DIM))
        outs[3].append(mk.reshape(batch, N_MEM, M_HEADS, M_HEAD_DIM))
        outs[4].append(mv.reshape(batch, N_MEM, M_HEADS, M_HEAD_DIM))

        ya, va, qb, kb, vb, qm = _mixer_inputs(xs, g_mix[l], w_in_b, g_av[l], ws_sample, bs_sample, g_qb[l],
                                               g_kb[l], g_qm[l], None, rows_s, "mixer_inputs_sample")
        yb, ym = _attention_sample(
            qb, kb, vb, cache_win_k[l], cache_win_v[l], bias_sample, qm, cache_mem_k[l], cache_mem_v[l], ds)
        rec = _out_and_router(ya, yb, ym, xs, w_out_b, g_ffn[l], w_router, rows_s, "out_router_sample")
        xs = _moe_dense(rec, g_ffn[l], wg_b, wu_b, wd_b, rows_s, "moe_sample")
        outs[5].append(kb.reshape(dec_batch, ds, B_HEADS, B_HEAD_DIM))
        outs[6].append(vb.reshape(dec_batch, ds, B_HEADS, B_HEAD_DIM))
        outs[7].append(va.reshape(dec_batch, ds, A_GROUPS, A_DIM))

    stacked = [jnp.stack(o, axis=0) for o in outs]
    return (xp.reshape(batch, seq, d), xs.reshape(dec_batch, ds, d), *stacked)
```

```python
import functools
import math

import numpy as np
import jax
import jax.numpy as jnp
from jax import lax
from jax.experimental import pallas as pl
from jax.experimental.pallas import tpu as pltpu

F32 = jnp.float32
BF16 = jnp.bfloat16

D_MODEL = 2048
A_GROUPS = 4
A_DIM = 128
A_WIDTH = A_GROUPS * A_DIM
CHUNK = 128
B_HEADS = 16
B_HEAD_DIM = 64
B_WIDTH = B_HEADS * B_HEAD_DIM
M_HEADS = 4
M_HEAD_DIM = 128
M_WIDTH = M_HEADS * M_HEAD_DIM
N_MEM = 256
DILATIONS = ((128, 1), (512, 4), (2048, 16))
N_DIL = len(DILATIONS)
KEYS_BACK = 128
RES = DILATIONS[1][1]
UNITS_IN_FLIGHT = 8
N_BUCKETS = 32
MAX_DISTANCE = 2048
N_GROUPS = 4
EXPERTS_PER_GROUP = 8
N_EXPERTS = N_GROUPS * EXPERTS_PER_GROUP
EPS = 1e-6

LANES = 128
V7X_VMEM_BYTES = 64 * 1024 * 1024
VMEM_CAP = V7X_VMEM_BYTES - 6 * 1024 * 1024
INV_SQRT2 = 0.7071067811865476
B_SCALE = B_HEAD_DIM ** -0.5
M_SCALE = M_HEAD_DIM ** -0.5
NEG_INF = float("-inf")

_NT = (((1,), (1,)), ((), ()))
_TN = (((0,), (0,)), ((), ()))


def _log2(n):
    assert n > 0 and n & (n - 1) == 0, n
    return n.bit_length() - 1


def _mm(a, b, dims=None):
    a, b = a.astype(BF16), b.astype(BF16)
    if dims is None:
        return jnp.dot(a, b, preferred_element_type=F32)
    return lax.dot_general(a, b, dims, preferred_element_type=F32)


def _vmem_limit(block_bytes, scratch_bytes=0):
    est = 2 * block_bytes + scratch_bytes + 16 * 1024 * 1024
    return int(min(max(est, 32 * 1024 * 1024), VMEM_CAP))


def _nbytes(shape, dtype):
    return int(np.prod(shape)) * jnp.dtype(dtype).itemsize


def _call(body, grid, in_specs, out_specs, out_shape, operands, name, semantics=None, scratch=()):
    outs = out_shape if isinstance(out_shape, (list, tuple)) else [out_shape]
    ospecs = out_specs if isinstance(out_specs, (list, tuple)) else [out_specs]
    blk = sum(_nbytes(s.block_shape, o.dtype) for s, o in zip(in_specs, operands))
    blk += sum(_nbytes(s.block_shape, o.dtype) for s, o in zip(ospecs, outs))
    scr = sum(_nbytes(s.shape, s.dtype) for s in scratch)
    if semantics is None:
        semantics = ("parallel",) * len(grid)
    return pl.pallas_call(
        body,
        grid=grid,
        in_specs=in_specs,
        out_specs=out_specs,
        out_shape=out_shape,
        scratch_shapes=list(scratch),
        compiler_params=pltpu.CompilerParams(dimension_semantics=semantics,
                                             vmem_limit_bytes=_vmem_limit(blk, scr)),
        name=name,
    )(*operands)


def _row_tile(rows):
    tile = 4 * LANES
    assert rows % tile == 0
    return tile


def _const_spec(shape):
    nd = len(shape)
    return pl.BlockSpec(shape, lambda *_: (0,) * nd, pipeline_mode=pl.Buffered(1))


def _row_norm(x, g_ref):
    return (x * lax.rsqrt(jnp.mean(x * x, axis=-1, keepdims=True) + EPS) * g_ref[...]).astype(BF16)


def _store_head_norm(z, g_ref, o_ref, head_dim):
    lane = lax.broadcasted_iota(jnp.int32, (z.shape[0], LANES), 1)
    for c0 in range(0, z.shape[1], LANES):
        blk = z[:, c0:c0 + LANES]
        sq = blk * blk
        if head_dim == LANES:
            inv = lax.rsqrt(jnp.mean(sq, axis=-1, keepdims=True) + EPS)
        else:
            lo = lane < head_dim
            s_lo = jnp.sum(jnp.where(lo, sq, 0.0), axis=-1, keepdims=True)
            s_hi = jnp.sum(jnp.where(lo, 0.0, sq), axis=-1, keepdims=True)
            inv = jnp.where(lo, lax.rsqrt(s_lo / head_dim + EPS), lax.rsqrt(s_hi / head_dim + EPS))
        o_ref[:, c0:c0 + LANES] = blk * inv * g_ref[:, c0:c0 + LANES]


def _memory_attend_heads(qm, mk_ref, mv_ref, ym_ref):
    for h in range(M_HEADS):
        c0 = h * M_HEAD_DIM
        if len(mk_ref.shape) == 3:
            mk, mv = mk_ref[:, h, :], mv_ref[:, h, :]
        else:
            mk, mv = mk_ref[:, c0:c0 + M_HEAD_DIM], mv_ref[:, c0:c0 + M_HEAD_DIM]
        s = _mm(qm[:, c0:c0 + M_HEAD_DIM], mk, _NT) * M_SCALE
        p = jnp.exp(s - jnp.max(s, axis=-1, keepdims=True))
        p = p / jnp.sum(p, axis=-1, keepdims=True)
        ym_ref[:, c0:c0 + M_HEAD_DIM] = _mm(p, mv)


def _mixer_inputs_body(*refs, cs, attend_memory):
    (x_ref, gmix_ref, w_ref, gav_ref, ws_ref, bs_ref, gq_ref, gk_ref, gqm_ref), refs = refs[:9], refs[9:]
    if attend_memory:
        (mk_ref, mv_ref), refs = refs[:2], refs[2:]
    ya_ref, va_ref, q_ref, k_ref, v_ref, m_ref, n_ref = refs[:7]
    o_b = 2 * A_WIDTH
    o_k = o_b + B_WIDTH
    o_v = o_k + B_WIDTH
    o_m = o_v + B_WIDTH
    n_ref[...] = _row_norm(x_ref[...], gmix_ref)
    tm = n_ref.shape[0]

    z = _mm(n_ref[...], w_ref[:, :o_b])
    uv = 0.5 * z * (1.0 + lax.erf(z * INV_SQRT2))
    for g in range(A_GROUPS):
        c0 = g * A_DIM
        u = uv[:, c0:c0 + A_DIM]
        v = uv[:, A_WIDTH + c0:A_WIDTH + c0 + A_DIM]
        va = v * lax.rsqrt(jnp.mean(v * v, axis=-1, keepdims=True) + EPS) * gav_ref[:, c0:c0 + A_DIM]
        va_ref[:, c0:c0 + A_DIM] = va
        for r0 in range(0, tm, cs):
            mixed = _mm(ws_ref[g], va[r0:r0 + cs]) + bs_ref[:, g:g + 1]
            ya_ref[r0:r0 + cs, c0:c0 + A_DIM] = u[r0:r0 + cs] * mixed

    _store_head_norm(_mm(n_ref[...], w_ref[:, o_b:o_k]), gq_ref, q_ref, B_HEAD_DIM)
    _store_head_norm(_mm(n_ref[...], w_ref[:, o_k:o_v]), gk_ref, k_ref, B_HEAD_DIM)
    v_ref[...] = _mm(n_ref[...], w_ref[:, o_v:o_m])
    if attend_memory:
        qm_ref = refs[7]
        _store_head_norm(_mm(n_ref[...], w_ref[:, o_m:]), gqm_ref, qm_ref, M_HEAD_DIM)
        _memory_attend_heads(qm_ref[...], mk_ref, mv_ref, m_ref)
    else:
        _store_head_norm(_mm(n_ref[...], w_ref[:, o_m:]), gqm_ref, m_ref, M_HEAD_DIM)


def _mixer_inputs(x, g_mix, w_in, g_av, ws_masked, bs_col, g_qb, g_kb, g_qm, memory, tm, name):
    rows, d = x.shape
    cs = ws_masked.shape[1]

    def tiled(gain, heads):
        return jnp.tile(gain.reshape(1, -1), (1, heads))

    def rs(width):
        return pl.BlockSpec((tm, width), lambda i: (i, 0))

    in_specs = [rs(d), _const_spec((1, d)), _const_spec(w_in.shape), _const_spec((1, A_WIDTH)),
                _const_spec((A_GROUPS, cs, cs)), _const_spec((cs, A_GROUPS)), _const_spec((1, B_WIDTH)),
                _const_spec((1, B_WIDTH)), _const_spec((1, M_WIDTH))]
    operands = [x, g_mix.reshape(1, d), w_in, g_av.reshape(1, A_WIDTH), ws_masked, bs_col,
                tiled(g_qb, B_HEADS), tiled(g_kb, B_HEADS), tiled(g_qm, M_HEADS)]
    scratch = [pltpu.VMEM((tm, d), BF16)]
    if memory is not None:
        mk, mv, seq = memory
        mem = pl.BlockSpec((N_MEM, M_WIDTH), lambda i: (i // (seq // tm), 0))
        in_specs += [mem, mem]
        operands += [mk, mv]
        scratch.append(pltpu.VMEM((tm, M_WIDTH), F32))
    widths = (A_WIDTH, A_WIDTH, B_WIDTH, B_WIDTH, B_WIDTH, M_WIDTH)
    return _call(functools.partial(_mixer_inputs_body, cs=cs, attend_memory=memory is not None), (rows // tm,),
                 in_specs, [rs(w) for w in widths], [jax.ShapeDtypeStruct((rows, w), F32) for w in widths],
                 operands, name, scratch=scratch)


def _mem_kv_body(x_ref, gmem_ref, w_ref, g_ref, mk_ref, mv_ref):
    kv = _mm(_row_norm(x_ref[...], gmem_ref), w_ref[...])
    _store_head_norm(kv[:, :M_WIDTH], g_ref, mk_ref, M_HEAD_DIM)
    mv_ref[...] = kv[:, M_WIDTH:]


def _memory_kv(mem, g_mem, w_kv, g_km):
    rows, d = mem.shape
    out = jax.ShapeDtypeStruct((rows, M_WIDTH), F32)
    spec = pl.BlockSpec((N_MEM, M_WIDTH), lambda i: (i, 0))
    return _call(_mem_kv_body, (rows // N_MEM,),
                 [pl.BlockSpec((N_MEM, d), lambda i: (i, 0)), _const_spec((1, d)), _const_spec((d, 2 * M_WIDTH)),
                  _const_spec((1, M_WIDTH))],
                 [spec, spec], [out, out],
                 (mem, g_mem.reshape(1, d), w_kv, jnp.tile(g_km.reshape(1, M_HEAD_DIM), (1, M_HEADS))), "memory_kv")


def _t5_bucket_np(dist):
    max_exact = N_BUCKETS // 2
    d = np.maximum(dist, 0)
    df = np.maximum(d, 1).astype(np.float64)
    large = max_exact + (np.log(df / max_exact) / math.log(MAX_DISTANCE / max_exact)
                         * (N_BUCKETS - max_exact)).astype(np.int32)
    large = np.minimum(large, N_BUCKETS - 1)
    return np.where(d < max_exact, d, large)


def _bias_by_distance(rel_bias, dists):
    onehot = np.zeros((N_BUCKETS, len(dists)), np.float32)
    onehot[_t5_bucket_np(np.asarray(dists)), np.arange(len(dists))] = 1.0
    return jnp.dot(rel_bias.astype(F32).T, jnp.asarray(onehot), precision=lax.Precision.HIGHEST)


def _prompt_bias_tables(rel_bias):
    wrap = 2 * KEYS_BACK + 1
    tabs = []
    for _, dil in DILATIONS:
        by_t = _bias_by_distance(rel_bias, [(KEYS_BACK - t) * dil for t in range(KEYS_BACK + 1)])
        row = jnp.concatenate([by_t, jnp.full((B_HEADS, wrap - KEYS_BACK - 1), NEG_INF, F32)], axis=1)
        flat = jnp.tile(row, (1, KEYS_BACK))[:, :KEYS_BACK * 2 * KEYS_BACK]
        tabs.append(flat.reshape(B_HEADS, KEYS_BACK, 2 * KEYS_BACK))
    return jnp.stack(tabs, axis=0)


def _sample_bias_tables(rel_bias, n_cache, ds):
    n_keys = n_cache + ds
    dist = n_cache + np.arange(ds)[:, None] - np.arange(n_keys)[None, :]
    by_key = _bias_by_distance(rel_bias, [max(n_cache + ds - 1 - c, 0) for c in range(n_keys + ds - 1)])
    bias = jnp.stack([by_key[:, ds - 1 - i:ds - 1 - i + n_keys] for i in range(ds)], axis=1)
    bias = bias.reshape(B_HEADS * ds, n_keys)
    tabs = []
    for window, dil in DILATIONS:
        used = (dist >= 0) & (dist % dil == 0) & (dist <= window)
        used = np.broadcast_to(used[None], (B_HEADS, ds, n_keys)).reshape(B_HEADS * ds, n_keys)
        tabs.append(jnp.where(used, bias, NEG_INF))
    return jnp.stack(tabs, axis=0)


def _dilated_prompt_body(q_ref, k_ref, v_ref, b_ref, o_ref, qd_ref, kd_ref, vd_ref, od_ref, lse_ref, *, seq):
    per = seq // RES
    lane = lax.broadcasted_iota(jnp.int32, (KEYS_BACK, LANES), 1)
    lo = lane < B_HEAD_DIM

    for c in range(RES):
        qd_ref[pl.ds(c * per, per), :] = q_ref[pl.ds(c, per, stride=RES), :] * B_SCALE
        kd_ref[pl.ds(c * per, per), :] = k_ref[pl.ds(c, per, stride=RES), :]
        vd_ref[pl.ds(c * per, per), :] = v_ref[pl.ds(c, per, stride=RES), :]

    def attend(units):
        nk = units[0][2].shape[0]
        off = 2 * KEYS_BACK - nk
        scores, biases = [], []
        for di, qv, kv, _, _ in units:
            kb = kv.astype(BF16)
            for h in range(2):
                qh = jnp.where(lo if h == 0 else jnp.logical_not(lo), qv, 0.0).astype(BF16)
                scores.append(lax.dot_general(qh, kb, _NT, preferred_element_type=F32))
                biases.append(b_ref[di, h, :, off:])
        s = jnp.concatenate(scores, axis=0) + jnp.concatenate(biases, axis=0)
        m = jnp.max(s, axis=-1, keepdims=True)
        p = jnp.exp(s - m)
        l = jnp.sum(p, axis=-1, keepdims=True)
        pn = (p * (1.0 / l)).astype(BF16)
        lse = m + jnp.log(l)
        for g, (di, _, _, vv, out_rows) in enumerate(units):
            vb = vv.astype(BF16)
            r0 = 2 * g * KEYS_BACK
            r1 = r0 + KEYS_BACK
            o0 = jnp.dot(pn[r0:r1], vb, preferred_element_type=F32)
            o1 = jnp.dot(pn[r1:r1 + KEYS_BACK], vb, preferred_element_type=F32)
            od_ref[di, out_rows, :] = jnp.where(lo, o0, o1)
            lse_ref[di, out_rows, :] = jnp.where(lo, lse[r0:r1], lse[r1:r1 + KEYS_BACK])

    def loop(n, group, make_unit):
        assert n % group == 0

        def step(t, carry):
            attend([make_unit(t * group + g) for g in range(group)])
            return carry
        lax.fori_loop(0, n // group, step, 0)

    blocks = seq // KEYS_BACK
    sub_blocks = per // KEYS_BACK

    def dense_unit(start):
        rows = pl.ds(start, KEYS_BACK)
        keys = rows if isinstance(start, int) and start == 0 else pl.ds(start - KEYS_BACK, 2 * KEYS_BACK)
        return 0, q_ref[rows, :] * B_SCALE, k_ref[keys, :], v_ref[keys, :], rows

    def mid_unit(start, first):
        rows = pl.ds(start, KEYS_BACK)
        keys = rows if first else pl.ds(start - KEYS_BACK, 2 * KEYS_BACK)
        return 1, qd_ref[rows, :], kd_ref[keys, :], vd_ref[keys, :], rows

    def wide_unit(r):
        rows = pl.ds((r & (RES - 1)) * per + (r >> _log2(RES)), KEYS_BACK, stride=RES)
        return 2, qd_ref[rows, :], kd_ref[rows, :], vd_ref[rows, :], rows

    attend([dense_unit(0)] + [mid_unit(c * per, True) for c in range(RES)])
    def group_of(n):
        return max(f for f in range(1, UNITS_IN_FLIGHT + 1) if n % f == 0)

    loop(blocks - 1, group_of(blocks - 1), lambda u: dense_unit(pl.multiple_of((u + 1) * KEYS_BACK, KEYS_BACK)))
    loop(RES * (sub_blocks - 1), group_of(RES * (sub_blocks - 1)),
         lambda u: mid_unit(pl.multiple_of((u & (RES - 1)) * per + (1 + (u >> _log2(RES))) * KEYS_BACK, KEYS_BACK),
                            False))
    loop(RES * RES, group_of(RES * RES), wide_unit)

    for c in range(RES):
        for j in range(sub_blocks):
            nat = pl.ds(c + RES * j * KEYS_BACK, KEYS_BACK, stride=RES)
            grp = pl.ds(c * per + j * KEYS_BACK, KEYS_BACK)
            lses = [lse_ref[0, nat, :], lse_ref[1, grp, :], lse_ref[2, grp, :]]
            outs = [od_ref[0, nat, :], od_ref[1, grp, :], od_ref[2, grp, :]]
            top = jnp.maximum(jnp.maximum(lses[0], lses[1]), lses[2])
            ws = [jnp.exp(x - top) for x in lses]
            inv = 1.0 / (ws[0] + ws[1] + ws[2])
            o_ref[nat, :] = (ws[0] * inv) * outs[0] + (ws[1] * inv) * outs[1] + (ws[2] * inv) * outs[2]


def _dilated_attention_prompt(q, k, v, bias_tabs, batch, seq):
    assert DILATIONS == ((KEYS_BACK, 1), (KEYS_BACK * RES, RES), (KEYS_BACK * RES * RES, RES * RES))
    assert seq == KEYS_BACK * RES * RES
    pairs = B_WIDTH // LANES
    blk = pl.BlockSpec((seq, LANES), lambda b, p: (b, p))
    return _call(functools.partial(_dilated_prompt_body, seq=seq), (batch, pairs),
                 [blk, blk, blk, pl.BlockSpec((N_DIL, 2, KEYS_BACK, 2 * KEYS_BACK), lambda b, p: (0, p, 0, 0))],
                 blk, jax.ShapeDtypeStruct((batch * seq, B_WIDTH), F32), (q, k, v, bias_tabs),
                 "dilated_attention_prompt",
                 scratch=[pltpu.VMEM((seq, LANES), F32)] * 3 + [pltpu.VMEM((N_DIL, seq, LANES), F32)] * 2)


def _sample_attn_body(q_ref, kn_ref, vn_ref, kt_ref, vt_ref, tf_ref, tc_ref, tn_ref, qm_ref, mk_ref, mv_ref,
                      yb_ref, ym_ref, *, ds, far):
    rows = B_HEADS * ds
    row = lax.broadcasted_iota(jnp.int32, (rows, B_WIDTH), 0)
    col = lax.broadcasted_iota(jnp.int32, (rows, B_WIDTH), 1)
    own = (row >> _log2(ds)) == (col >> _log2(B_HEAD_DIM))
    q = q_ref[...] * B_SCALE
    qbd = jnp.where(own, jnp.concatenate([q] * B_HEADS, axis=0), 0.0).astype(BF16)
    s_f = _mm(qbd, kt_ref[:, :far]) + tf_ref[...]
    s_c = _mm(qbd, kt_ref[:, far:])
    s_n = _mm(qbd, kn_ref[...], _NT)
    pc, pn, lses = [], [], []
    p_far = None
    for di in range(N_DIL):
        widest = di == N_DIL - 1
        sc = s_c + tc_ref[di]
        sn = s_n + tn_ref[di]
        m = jnp.maximum(jnp.max(sc, axis=-1, keepdims=True), jnp.max(sn, axis=-1, keepdims=True))
        if widest:
            m = jnp.maximum(m, jnp.max(s_f, axis=-1, keepdims=True))
        ec = jnp.exp(sc - m)
        en = jnp.exp(sn - m)
        l = jnp.sum(ec, axis=-1, keepdims=True) + jnp.sum(en, axis=-1, keepdims=True)
        if widest:
            ef = jnp.exp(s_f - m)
            l = l + jnp.sum(ef, axis=-1, keepdims=True)
        inv = 1.0 / l
        if widest:
            p_far = (ef * inv).astype(BF16)
        pc.append((ec * inv).astype(BF16))
        pn.append((en * inv).astype(BF16))
        lses.append(m + jnp.log(l))
    o = (_mm(jnp.concatenate(pc, axis=0), vt_ref[:, far:], _NT)
         + _mm(jnp.concatenate(pn, axis=0), vn_ref[...]))
    o_far = _mm(p_far, vt_ref[:, :far], _NT)
    top = jnp.maximum(jnp.maximum(lses[0], lses[1]), lses[2])
    ws = [jnp.exp(x - top) for x in lses]
    den = ws[0] + ws[1] + ws[2]
    mixed = (ws[N_DIL - 1] / den) * (o[(N_DIL - 1) * rows:] + o_far)
    for di in range(N_DIL - 1):
        mixed = mixed + (ws[di] / den) * o[di * rows:(di + 1) * rows]
    mixed = jnp.where(own, mixed, 0.0)
    yb = mixed[0:ds]
    for h in range(1, B_HEADS):
        yb = yb + mixed[h * ds:(h + 1) * ds]
    yb_ref[...] = yb
    _memory_attend_heads(qm_ref[...], mk_ref, mv_ref, ym_ref)


def _attention_sample(q, k_new, v_new, cache_k, cache_v, tabs, qm, cache_mk, cache_mv, ds):
    batch, n_cache = cache_k.shape[:2]
    rows = B_HEADS * ds
    far = n_cache - DILATIONS[-2][0]
    assert N_DIL == 3 and rows == LANES and far % LANES == 0
    assert all(w0 <= w1 for (w0, _), (w1, _) in zip(DILATIONS, DILATIONS[1:]))

    def keys_minor(x):
        return jnp.transpose(x, (0, 2, 3, 1)).reshape(batch * B_WIDTH, n_cache)

    new = pl.BlockSpec((ds, B_WIDTH), lambda b: (b, 0))
    cache = pl.BlockSpec((B_WIDTH, n_cache), lambda b: (b, 0))
    mem = pl.BlockSpec((N_MEM, M_HEADS, M_HEAD_DIM), lambda b: (b, 0, 0))
    cache_mk = cache_mk.reshape(batch * N_MEM, M_HEADS, M_HEAD_DIM)
    cache_mv = cache_mv.reshape(batch * N_MEM, M_HEADS, M_HEAD_DIM)
    qm_spec = pl.BlockSpec((ds, M_WIDTH), lambda b: (b, 0))
    return _call(functools.partial(_sample_attn_body, ds=ds, far=far), (batch,),
                 [new, new, new, cache, cache, _const_spec((rows, far)), _const_spec((N_DIL, rows, n_cache - far)),
                  _const_spec((N_DIL, rows, ds)), qm_spec, mem, mem],
                 [new, qm_spec],
                 [jax.ShapeDtypeStruct((batch * ds, B_WIDTH), F32), jax.ShapeDtypeStruct((batch * ds, M_WIDTH), F32)],
                 (q, k_new, v_new, keys_minor(cache_k), keys_minor(cache_v), tabs[N_DIL - 1, :, :far],
                  tabs[:, :, far:n_cache], tabs[:, :, n_cache:], qm, cache_mk, cache_mv),
                 "attention_sample")


def _out_router_body(ya_ref, yb_ref, ym_ref, x_ref, wo_ref, gf_ref, wr_ref, rec_ref):
    o_b = A_WIDTH
    o_m = A_WIDTH + B_WIDTH
    d = x_ref.shape[1]
    mix = (_mm(ya_ref[...], wo_ref[0:o_b, :]) + _mm(yb_ref[...], wo_ref[o_b:o_m, :])
           + _mm(ym_ref[...], wo_ref[o_m:, :]))
    h = x_ref[...] + mix
    rec_ref[:, :d] = h
    logits = _mm(_row_norm(h, gf_ref), wr_ref[...])
    lane_i = lax.broadcasted_iota(jnp.int32, logits.shape, 1)
    lane = lane_i.astype(F32)
    big = float(LANES)

    def first_argmax(vals):
        top = jnp.max(vals, axis=-1, keepdims=True)
        return top, jnp.min(jnp.where(vals == top, lane, big), axis=-1, keepdims=True)

    is_group = (lane_i >= N_EXPERTS) & (lane_i < N_EXPERTS + N_GROUPS)
    _, g_lane = first_argmax(jnp.where(is_group, logits, NEG_INF))
    grp = g_lane.astype(jnp.int32) - N_EXPERTS
    in_grp = (lane_i < N_EXPERTS) & ((lane_i >> _log2(EXPERTS_PER_GROUP)) == grp)
    e_log = jnp.where(in_grp, logits, NEG_INF)
    v1, i1 = first_argmax(e_log)
    v2, i2 = first_argmax(jnp.where(lane == i1, NEG_INF, e_log))
    e2 = jnp.exp(v2 - v1)
    den = 1.0 + e2
    rec_ref[:, d:] = (jnp.where(lane == i1, 1.0 / den, 0.0) + jnp.where(lane == i2, e2 / den, 0.0)
                      + jnp.where(lane_i == N_EXPERTS, grp.astype(F32), 0.0))


def _out_and_router(ya, yb, ym, x, w_out, g_ffn, w_router, tm, name):
    rows, d = x.shape

    def rs(width):
        return pl.BlockSpec((tm, width), lambda i: (i, 0))

    return _call(_out_router_body, (rows // tm,),
                 [rs(A_WIDTH), rs(B_WIDTH), rs(M_WIDTH), rs(d), _const_spec((d, d)), _const_spec((1, d)),
                  _const_spec((d, LANES))],
                 rs(d + LANES), jax.ShapeDtypeStruct((rows, d + LANES), F32),
                 (ya, yb, ym, x, w_out, g_ffn.reshape(1, d), w_router), name)


def _expert_ffn(n2, gates, expert_lane, wg_ref, wu_ref, wd_ref):
    hg = jnp.dot(n2, wg_ref[0], preferred_element_type=F32)
    hu = jnp.dot(n2, wu_ref[0], preferred_element_type=F32)
    lane = lax.broadcasted_iota(jnp.int32, gates.shape, 1)
    g = jnp.sum(jnp.where(lane == expert_lane, gates, 0.0), axis=-1, keepdims=True)
    act = hg * jax.nn.sigmoid(hg) * hu * g
    return jnp.dot(act.astype(BF16), wd_ref[0], preferred_element_type=F32)


def _moe_dense_body(rec_ref, gf_ref, wg_ref, wu_ref, wd_ref, o_ref, n2_ref):
    e = pl.program_id(1)
    d = o_ref.shape[1]

    @pl.when(e == 0)
    def _():
        h = rec_ref[:, :d]
        o_ref[...] = h
        n2_ref[...] = _row_norm(h, gf_ref)

    o_ref[...] += _expert_ffn(n2_ref[...], rec_ref[:, d:], e, wg_ref, wu_ref, wd_ref)


def _moe_dense(rec, g_ffn, w_gate, w_up, w_down, tm, name):
    rows = rec.shape[0]
    d, ff = w_gate.shape[1:]
    wspec = lambda a, b: pl.BlockSpec((1, a, b), lambda i, e: (e, 0, 0))
    return _call(_moe_dense_body, (rows // tm, N_EXPERTS),
                 [pl.BlockSpec((tm, d + LANES), lambda i, e: (i, 0)), _const_spec((1, d)),
                  wspec(d, ff), wspec(d, ff), wspec(ff, d)],
                 pl.BlockSpec((tm, d), lambda i, e: (i, 0)), jax.ShapeDtypeStruct((rows, d), F32),
                 (rec, g_ffn.reshape(1, d), w_gate, w_up, w_down), name,
                 semantics=("parallel", "arbitrary"), scratch=[pltpu.VMEM((tm, d), BF16)])


def _group_sort_plan(rec, tm):
    n, width = rec.shape
    d = width - LANES
    n_tiles = n // tm + N_GROUPS
    grp = rec[:, d + N_EXPERTS].astype(jnp.int32)
    order = jnp.argsort(grp, stable=True).astype(jnp.int32)
    counts = jnp.sum((grp[:, None] == jnp.arange(N_GROUPS)[None, :]).astype(jnp.int32), axis=0)
    tiles_per = (counts + tm - 1) // tm
    tile_end = jnp.cumsum(tiles_per)
    tile_start = tile_end - tiles_per
    tok_start = jnp.cumsum(counts) - counts
    n_used = tile_end[-1]
    t = jnp.arange(n_tiles, dtype=jnp.int32)
    tgrp = jnp.minimum(jnp.sum((t[:, None] >= tile_end[None, :]).astype(jnp.int32), axis=1), N_GROUPS - 1)
    row0 = (t - tile_start[tgrp]) * tm
    nvalid = jnp.where(t < n_used, jnp.clip(counts[tgrp] - row0, 0, tm), 0).astype(jnp.int32)
    k = row0[:, None] + jnp.arange(tm, dtype=jnp.int32)[None, :]
    real = jnp.arange(tm, dtype=jnp.int32)[None, :] < nvalid[:, None]
    src = jnp.where(real, order[jnp.clip(tok_start[tgrp][:, None] + k, 0, n - 1)], -1)
    return src.reshape(-1).astype(jnp.int32), tgrp.astype(jnp.int32), nvalid, n_used.reshape(1).astype(jnp.int32)


def _moe_grouped_body(src_ref, tgrp_ref, nvalid_ref, nused_ref, rec_hbm, gf_ref, wg_ref, wu_ref, wd_ref, y_hbm,
                      rec_buf, n2_ref, acc_ref, y_buf, gsem, ssem, *, tm):
    i = pl.program_id(0)
    e = pl.program_id(1)
    d = y_buf.shape[1]
    n_used = nused_ref[0]
    slot = i % 2

    def gather_rows(tile, slot_):
        def row(r, c):
            tok = jnp.maximum(src_ref[tile * tm + r], 0)
            pltpu.make_async_copy(rec_hbm.at[pl.ds(tok, 1)], rec_buf.at[slot_, pl.ds(r, 1)], gsem.at[slot_]).start()
            return c
        lax.fori_loop(0, tm, row, 0, unroll=8)

    def wait_gather(slot_):
        pltpu.make_async_copy(rec_hbm.at[pl.ds(0, tm)], rec_buf.at[slot_], gsem.at[slot_]).wait()

    def scatter_copy(tile, r):
        tok = src_ref[tile * tm + r]
        return pltpu.make_async_copy(y_buf.at[pl.ds(r, 1)], y_hbm.at[pl.ds(tok, 1)], ssem.at[0])

    def scatter_rows(tile, wait):
        def row(r, c):
            cp = scatter_copy(tile, r)
            if wait:
                cp.wait()
            else:
                cp.start()
            return c

        full = nvalid_ref[tile] == tm

        @pl.when(full)
        def _():
            if wait:
                pltpu.make_async_copy(y_buf, y_hbm.at[pl.ds(0, tm)], ssem.at[0]).wait()
            else:
                lax.fori_loop(0, tm, row, 0, unroll=8)

        @pl.when(jnp.logical_not(full))
        def _():
            lax.fori_loop(0, nvalid_ref[tile], row, 0)

    @pl.when(i < n_used)
    def _():
        @pl.when(e == 0)
        def _():
            @pl.when(i == 0)
            def _():
                gather_rows(0, 0)

            wait_gather(slot)

            @pl.when(i + 1 < n_used)
            def _():
                gather_rows(i + 1, 1 - slot)

            n2_ref[...] = _row_norm(rec_buf[slot, :, :d], gf_ref)
            acc_ref[...] = jnp.zeros_like(acc_ref)

        lane0 = tgrp_ref[i] * EXPERTS_PER_GROUP
        acc_ref[...] += _expert_ffn(n2_ref[...], rec_buf[slot, :, d:], lane0 + e, wg_ref, wu_ref, wd_ref)

        @pl.when(e == EXPERTS_PER_GROUP - 1)
        def _():
            @pl.when(i > 0)
            def _():
                scatter_rows(i - 1, True)

            y_buf[...] = rec_buf[slot, :, :d] + acc_ref[...]
            scatter_rows(i, False)

            @pl.when(i == n_used - 1)
            def _():
                scatter_rows(i, True)


def _moe_grouped(rec, g_ffn, w_gate, w_up, w_down, tm, name):
    n = rec.shape[0]
    d, ff = w_gate.shape[1:]
    src, tgrp, nvalid, n_used = _group_sort_plan(rec, tm)
    n_tiles = tgrp.shape[0]

    def widx(i, e, src_ref, tgrp_ref, nvalid_ref, nused_ref):
        last = nused_ref[0] - 1
        live = i <= last
        return (tgrp_ref[jnp.minimum(i, last)] * EXPERTS_PER_GROUP
                + jnp.where(live, e, EXPERTS_PER_GROUP - 1), 0, 0)

    scratch = [pltpu.VMEM((2, tm, d + LANES), F32), pltpu.VMEM((tm, d), BF16), pltpu.VMEM((tm, d), F32),
               pltpu.VMEM((tm, d), F32), pltpu.SemaphoreType.DMA((2,)), pltpu.SemaphoreType.DMA((1,))]
    blk = 3 * _nbytes((d, ff), BF16) + _nbytes((1, d), F32)
    scr = sum(_nbytes(s.shape, s.dtype) for s in scratch[:4])
    return pl.pallas_call(
        functools.partial(_moe_grouped_body, tm=tm),
        grid_spec=pltpu.PrefetchScalarGridSpec(
            num_scalar_prefetch=4,
            grid=(n_tiles, EXPERTS_PER_GROUP),
            in_specs=[pl.BlockSpec(memory_space=pl.ANY),
                      pl.BlockSpec((1, d), lambda i, e, *_: (0, 0)),
                      pl.BlockSpec((1, d, ff), widx), pl.BlockSpec((1, d, ff), widx), pl.BlockSpec((1, ff, d), widx)],
            out_specs=pl.BlockSpec(memory_space=pl.ANY),
            scratch_shapes=scratch),
        out_shape=jax.ShapeDtypeStruct((n, d), F32),
        compiler_params=pltpu.CompilerParams(dimension_semantics=("arbitrary", "arbitrary"),
                                             vmem_limit_bytes=_vmem_limit(blk, scr)),
        name=name,
    )(src, tgrp, nvalid, n_used, rec, g_ffn.reshape(1, d), w_gate, w_up, w_down)


def kernel(x_prompt, x_sample, mem_prompt, cache_win_k, cache_win_v, cache_mem_k, cache_mem_v, rel_bias, g_mix, w_in, g_av, w_s, b_s, g_qb, g_kb, g_qm, g_km, g_mem, w_mem_kv, w_out, g_ffn, w_router_group, w_router_expert, w_gate, w_up, w_down):
    batch, seq, d = x_prompt.shape
    dec_batch, ds, _ = x_sample.shape
    depth = w_in.shape[0]
    n_cache = cache_win_k.shape[2]
    assert seq % (KEYS_BACK * DILATIONS[-1][1]) == 0 and seq <= DILATIONS[-1][0] and seq % CHUNK == 0
    assert n_cache >= DILATIONS[-1][0] and ds <= CHUNK

    xp = x_prompt.reshape(batch * seq, d)
    xs = x_sample.reshape(dec_batch * ds, d)
    mem = mem_prompt.reshape(batch * N_MEM, d)
    rows_s = dec_batch * ds
    tm = _row_tile(batch * seq)

    bias_prompt = _prompt_bias_tables(rel_bias)
    bias_sample = _sample_bias_tables(rel_bias, n_cache, ds)
    causal = np.tril(np.ones((CHUNK, CHUNK), bool))
    same_row = (np.arange(rows_s)[:, None] // ds) == (np.arange(rows_s)[None, :] // ds)

    outs = [[] for _ in range(8)]
    for l in range(depth):
        w_in_b = w_in[l].astype(BF16)
        w_out_b = w_out[l].astype(BF16)
        w_router = jnp.concatenate(
            [jnp.transpose(w_router_expert[l], (1, 0, 2)).reshape(d, N_EXPERTS), w_router_group[l],
             jnp.zeros((d, LANES - N_EXPERTS - N_GROUPS), F32)], axis=1).astype(BF16)
        wg_b, wu_b, wd_b = w_gate[l].astype(BF16), w_up[l].astype(BF16), w_down[l].astype(BF16)
        ws_prompt = jnp.where(causal, w_s[l], 0.0)
        bs_prompt = b_s[l].T
        ws_small = jnp.where(causal[:ds, :ds], w_s[l][:, :ds, :ds], 0.0)
        ws_sample = jnp.where(same_row, jnp.tile(ws_small, (1, dec_batch, dec_batch)), 0.0)
        bs_sample = jnp.tile(b_s[l][:, :ds].T, (dec_batch, 1))

        mk, mv = _memory_kv(mem, g_mem[l], w_mem_kv[l].astype(BF16), g_km[l])
        ya, va, qb, kb, vb, ym = _mixer_inputs(xp, g_mix[l], w_in_b, g_av[l], ws_prompt, bs_prompt, g_qb[l],
                                               g_kb[l], g_qm[l], (mk, mv, seq), tm, "mixer_inputs_prompt")
        yb = _dilated_attention_prompt(qb, kb, vb, bias_prompt, batch, seq)
        rec = _out_and_router(ya, yb, ym, xp, w_out_b, g_ffn[l], w_router, tm // 2, "out_router_prompt")
        xp = _moe_grouped(rec, g_ffn[l], wg_b, wu_b, wd_b, tm, "moe_prompt")
        wp = min(DILATIONS[-1][0], seq)
        outs[0].append(kb.reshape(batch, seq, B_HEADS, B_HEAD_DIM)[:, seq - wp:])
        outs[1].append(vb.reshape(batch, seq, B_HEADS, B_HEAD_DIM)[:, seq - wp:])
        outs[2].append(va.reshape(batch, seq, A_WIDTH)[:, seq - CHUNK:].reshape(batch, CHUNK, A_GROUPS, A_DIM))
        outs[3].append(mk.reshape(batch, N_MEM, M_HEADS, M_HEAD_DIM))
        outs[4].append(mv.reshape(batch, N_MEM, M_HEADS, M_HEAD_DIM))

        ya, va, qb, kb, vb, qm = _mixer_inputs(xs, g_mix[l], w_in_b, g_av[l], ws_sample, bs_sample, g_qb[l],
                                               g_kb[l], g_qm[l], None, rows_s, "mixer_inputs_sample")
        yb, ym = _attention_sample(
            qb, kb, vb, cache_win_k[l], cache_win_v[l], bias_sample, qm, cache_mem_k[l], cache_mem_v[l], ds)
        rec = _out_and_router(ya, yb, ym, xs, w_out_b, g_ffn[l], w_router, rows_s, "out_router_sample")
        xs = _moe_dense(rec, g_ffn[l], wg_b, wu_b, wd_b, rows_s, "moe_sample")
        outs[5].append(kb.reshape(dec_batch, ds, B_HEADS, B_HEAD_DIM))
        outs[6].append(vb.reshape(dec_batch, ds, B_HEADS, B_HEAD_DIM))
        outs[7].append(va.reshape(dec_batch, ds, A_GROUPS, A_DIM))

    stacked = [jnp.stack(o, axis=0) for o in outs]
    return (xp.reshape(batch, seq, d), xs.reshape(dec_batch, ds, d), *stacked)
```

```python
import functools
import math

import numpy as np
import jax
import jax.numpy as jnp
from jax import lax
from jax.experimental import pallas as pl
from jax.experimental.pallas import tpu as pltpu

F32 = jnp.float32
BF16 = jnp.bfloat16

D_MODEL = 2048
A_GROUPS = 4
A_DIM = 128
A_WIDTH = A_GROUPS * A_DIM
CHUNK = 128
B_HEADS = 16
B_HEAD_DIM = 64
B_WIDTH = B_HEADS * B_HEAD_DIM
M_HEADS = 4
M_HEAD_DIM = 128
M_WIDTH = M_HEADS * M_HEAD_DIM
N_MEM = 256
DILATIONS = ((128, 1), (512, 4), (2048, 16))
N_DIL = len(DILATIONS)
KEYS_BACK = 128
RES = DILATIONS[1][1]
UNITS_IN_FLIGHT = 16
N_BUCKETS = 32
MAX_DISTANCE = 2048
N_GROUPS = 4
EXPERTS_PER_GROUP = 8
N_EXPERTS = N_GROUPS * EXPERTS_PER_GROUP
EPS = 1e-6

LANES = 128
V7X_VMEM_BYTES = 64 * 1024 * 1024
VMEM_CAP = V7X_VMEM_BYTES - 6 * 1024 * 1024
INV_SQRT2 = 0.7071067811865476
B_SCALE = B_HEAD_DIM ** -0.5
M_SCALE = M_HEAD_DIM ** -0.5
NEG_INF = float("-inf")

_NT = (((1,), (1,)), ((), ()))
_TN = (((0,), (0,)), ((), ()))


def _log2(n):
    assert n > 0 and n & (n - 1) == 0, n
    return n.bit_length() - 1


def _mm(a, b, dims=None):
    a, b = a.astype(BF16), b.astype(BF16)
    if dims is None:
        return jnp.dot(a, b, preferred_element_type=F32)
    return lax.dot_general(a, b, dims, preferred_element_type=F32)


def _vmem_limit(block_bytes, scratch_bytes=0):
    est = 2 * block_bytes + scratch_bytes + 16 * 1024 * 1024
    return int(min(max(est, 32 * 1024 * 1024), VMEM_CAP))


def _nbytes(shape, dtype):
    return int(np.prod(shape)) * jnp.dtype(dtype).itemsize


def _call(body, grid, in_specs, out_specs, out_shape, operands, name, semantics=None, scratch=()):
    outs = out_shape if isinstance(out_shape, (list, tuple)) else [out_shape]
    ospecs = out_specs if isinstance(out_specs, (list, tuple)) else [out_specs]
    blk = sum(_nbytes(s.block_shape, o.dtype) for s, o in zip(in_specs, operands))
    blk += sum(_nbytes(s.block_shape, o.dtype) for s, o in zip(ospecs, outs))
    scr = sum(_nbytes(s.shape, s.dtype) for s in scratch)
    if semantics is None:
        semantics = ("parallel",) * len(grid)
    return pl.pallas_call(
        body,
        grid=grid,
        in_specs=in_specs,
        out_specs=out_specs,
        out_shape=out_shape,
        scratch_shapes=list(scratch),
        compiler_params=pltpu.CompilerParams(dimension_semantics=semantics,
                                             vmem_limit_bytes=_vmem_limit(blk, scr)),
        name=name,
    )(*operands)


def _row_tile(rows):
    tile = 4 * LANES
    assert rows % tile == 0
    return tile


def _const_spec(shape):
    nd = len(shape)
    return pl.BlockSpec(shape, lambda *_: (0,) * nd, pipeline_mode=pl.Buffered(1))


def _row_norm(x, g_ref):
    return (x * lax.rsqrt(jnp.mean(x * x, axis=-1, keepdims=True) + EPS) * g_ref[...]).astype(BF16)


def _store_head_norm(z, g_ref, o_ref, head_dim):
    lane = lax.broadcasted_iota(jnp.int32, (z.shape[0], LANES), 1)
    for c0 in range(0, z.shape[1], LANES):
        blk = z[:, c0:c0 + LANES]
        sq = blk * blk
        if head_dim == LANES:
            inv = lax.rsqrt(jnp.mean(sq, axis=-1, keepdims=True) + EPS)
        else:
            lo = lane < head_dim
            s_lo = jnp.sum(jnp.where(lo, sq, 0.0), axis=-1, keepdims=True)
            s_hi = jnp.sum(jnp.where(lo, 0.0, sq), axis=-1, keepdims=True)
            inv = jnp.where(lo, lax.rsqrt(s_lo / head_dim + EPS), lax.rsqrt(s_hi / head_dim + EPS))
        o_ref[:, c0:c0 + LANES] = blk * inv * g_ref[:, c0:c0 + LANES]


def _memory_attend_heads(qm, mk_ref, mv_ref, ym_ref):
    for h in range(M_HEADS):
        c0 = h * M_HEAD_DIM
        if len(mk_ref.shape) == 3:
            mk, mv = mk_ref[:, h, :], mv_ref[:, h, :]
        else:
            mk, mv = mk_ref[:, c0:c0 + M_HEAD_DIM], mv_ref[:, c0:c0 + M_HEAD_DIM]
        s = _mm(qm[:, c0:c0 + M_HEAD_DIM], mk, _NT) * M_SCALE
        p = jnp.exp(s - jnp.max(s, axis=-1, keepdims=True))
        p = p / jnp.sum(p, axis=-1, keepdims=True)
        ym_ref[:, c0:c0 + M_HEAD_DIM] = _mm(p, mv)


def _mixer_inputs_body(*refs, cs, attend_memory):
    (x_ref, gmix_ref, w_ref, gav_ref, ws_ref, bs_ref, gq_ref, gk_ref, gqm_ref), refs = refs[:9], refs[9:]
    if attend_memory:
        (mk_ref, mv_ref), refs = refs[:2], refs[2:]
    ya_ref, va_ref, q_ref, k_ref, v_ref, m_ref, n_ref = refs[:7]
    o_b = 2 * A_WIDTH
    o_k = o_b + B_WIDTH
    o_v = o_k + B_WIDTH
    o_m = o_v + B_WIDTH
    n_ref[...] = _row_norm(x_ref[...], gmix_ref)
    tm = n_ref.shape[0]

    z = _mm(n_ref[...], w_ref[:, :o_b])
    uv = 0.5 * z * (1.0 + lax.erf(z * INV_SQRT2))
    for g in range(A_GROUPS):
        c0 = g * A_DIM
        u = uv[:, c0:c0 + A_DIM]
        v = uv[:, A_WIDTH + c0:A_WIDTH + c0 + A_DIM]
        va = v * lax.rsqrt(jnp.mean(v * v, axis=-1, keepdims=True) + EPS) * gav_ref[:, c0:c0 + A_DIM]
        va_ref[:, c0:c0 + A_DIM] = va
        for r0 in range(0, tm, cs):
            mixed = _mm(ws_ref[g], va[r0:r0 + cs]) + bs_ref[:, g:g + 1]
            ya_ref[r0:r0 + cs, c0:c0 + A_DIM] = u[r0:r0 + cs] * mixed

    _store_head_norm(_mm(n_ref[...], w_ref[:, o_b:o_k]), gq_ref, q_ref, B_HEAD_DIM)
    _store_head_norm(_mm(n_ref[...], w_ref[:, o_k:o_v]), gk_ref, k_ref, B_HEAD_DIM)
    v_ref[...] = _mm(n_ref[...], w_ref[:, o_v:o_m])
    if attend_memory:
        qm_ref = refs[7]
        _store_head_norm(_mm(n_ref[...], w_ref[:, o_m:]), gqm_ref, qm_ref, M_HEAD_DIM)
        _memory_attend_heads(qm_ref[...], mk_ref, mv_ref, m_ref)
    else:
        _store_head_norm(_mm(n_ref[...], w_ref[:, o_m:]), gqm_ref, m_ref, M_HEAD_DIM)


def _mixer_inputs(x, g_mix, w_in, g_av, ws_masked, bs_col, g_qb, g_kb, g_qm, memory, tm, name):
    rows, d = x.shape
    cs = ws_masked.shape[1]

    def tiled(gain, heads):
        return jnp.tile(gain.reshape(1, -1), (1, heads))

    def rs(width):
        return pl.BlockSpec((tm, width), lambda i: (i, 0))

    in_specs = [rs(d), _const_spec((1, d)), _const_spec(w_in.shape), _const_spec((1, A_WIDTH)),
                _const_spec((A_GROUPS, cs, cs)), _const_spec((cs, A_GROUPS)), _const_spec((1, B_WIDTH)),
                _const_spec((1, B_WIDTH)), _const_spec((1, M_WIDTH))]
    operands = [x, g_mix.reshape(1, d), w_in, g_av.reshape(1, A_WIDTH), ws_masked, bs_col,
                tiled(g_qb, B_HEADS), tiled(g_kb, B_HEADS), tiled(g_qm, M_HEADS)]
    scratch = [pltpu.VMEM((tm, d), BF16)]
    if memory is not None:
        mk, mv, seq = memory
        mem = pl.BlockSpec((N_MEM, M_WIDTH), lambda i: (i // (seq // tm), 0))
        in_specs += [mem, mem]
        operands += [mk, mv]
        scratch.append(pltpu.VMEM((tm, M_WIDTH), F32))
    widths = (A_WIDTH, A_WIDTH, B_WIDTH, B_WIDTH, B_WIDTH, M_WIDTH)
    return _call(functools.partial(_mixer_inputs_body, cs=cs, attend_memory=memory is not None), (rows // tm,),
                 in_specs, [rs(w) for w in widths], [jax.ShapeDtypeStruct((rows, w), F32) for w in widths],
                 operands, name, scratch=scratch)


def _mem_kv_body(x_ref, gmem_ref, w_ref, g_ref, mk_ref, mv_ref):
    kv = _mm(_row_norm(x_ref[...], gmem_ref), w_ref[...])
    _store_head_norm(kv[:, :M_WIDTH], g_ref, mk_ref, M_HEAD_DIM)
    mv_ref[...] = kv[:, M_WIDTH:]


def _memory_kv(mem, g_mem, w_kv, g_km):
    rows, d = mem.shape
    out = jax.ShapeDtypeStruct((rows, M_WIDTH), F32)
    spec = pl.BlockSpec((N_MEM, M_WIDTH), lambda i: (i, 0))
    return _call(_mem_kv_body, (rows // N_MEM,),
                 [pl.BlockSpec((N_MEM, d), lambda i: (i, 0)), _const_spec((1, d)), _const_spec((d, 2 * M_WIDTH)),
                  _const_spec((1, M_WIDTH))],
                 [spec, spec], [out, out],
                 (mem, g_mem.reshape(1, d), w_kv, jnp.tile(g_km.reshape(1, M_HEAD_DIM), (1, M_HEADS))), "memory_kv")


def _t5_bucket_np(dist):
    max_exact = N_BUCKETS // 2
    d = np.maximum(dist, 0)
    df = np.maximum(d, 1).astype(np.float64)
    large = max_exact + (np.log(df / max_exact) / math.log(MAX_DISTANCE / max_exact)
                         * (N_BUCKETS - max_exact)).astype(np.int32)
    large = np.minimum(large, N_BUCKETS - 1)
    return np.where(d < max_exact, d, large)


def _bias_by_distance(rel_bias, dists):
    onehot = np.zeros((N_BUCKETS, len(dists)), np.float32)
    onehot[_t5_bucket_np(np.asarray(dists)), np.arange(len(dists))] = 1.0
    return jnp.dot(rel_bias.astype(F32).T, jnp.asarray(onehot), precision=lax.Precision.HIGHEST)


def _prompt_bias_tables(rel_bias):
    wrap = 2 * KEYS_BACK + 1
    tabs = []
    for _, dil in DILATIONS:
        by_t = _bias_by_distance(rel_bias, [(KEYS_BACK - t) * dil for t in range(KEYS_BACK + 1)])
        row = jnp.concatenate([by_t, jnp.full((B_HEADS, wrap - KEYS_BACK - 1), NEG_INF, F32)], axis=1)
        flat = jnp.tile(row, (1, KEYS_BACK))[:, :KEYS_BACK * 2 * KEYS_BACK]
        tabs.append(flat.reshape(B_HEADS, KEYS_BACK, 2 * KEYS_BACK))
    return jnp.stack(tabs, axis=0)


def _sample_bias_tables(rel_bias, n_cache, ds):
    n_keys = n_cache + ds
    dist = n_cache + np.arange(ds)[:, None] - np.arange(n_keys)[None, :]
    by_key = _bias_by_distance(rel_bias, [max(n_cache + ds - 1 - c, 0) for c in range(n_keys + ds - 1)])
    bias = jnp.stack([by_key[:, ds - 1 - i:ds - 1 - i + n_keys] for i in range(ds)], axis=1)
    bias = bias.reshape(B_HEADS * ds, n_keys)
    tabs = []
    for window, dil in DILATIONS:
        used = (dist >= 0) & (dist % dil == 0) & (dist <= window)
        used = np.broadcast_to(used[None], (B_HEADS, ds, n_keys)).reshape(B_HEADS * ds, n_keys)
        tabs.append(jnp.where(used, bias, NEG_INF))
    return jnp.stack(tabs, axis=0)


def _dilated_prompt_body(q_ref, k_ref, v_ref, b_ref, o_ref, kt_ref, vt_ref, qd_ref, kd_ref, vd_ref, od_ref, m_ref,
                         l_ref, *, seq):
    per = seq // RES
    lane = lax.broadcasted_iota(jnp.int32, (KEYS_BACK, LANES), 1)
    lo = lane < B_HEAD_DIM

    for c in range(RES):
        qd_ref[pl.ds(c * per, per), :] = q_ref[pl.ds(c, per, stride=RES), :] * B_SCALE
        kd_ref[pl.ds(c * per, per), :] = k_ref[pl.ds(c, per, stride=RES), :]
        vd_ref[pl.ds(c * per, per), :] = v_ref[pl.ds(c, per, stride=RES), :]

    for r0 in range(0, seq, LANES):
        kt_ref[:, r0:r0 + LANES] = k_ref[r0:r0 + LANES, :].T
        vt_ref[:, r0:r0 + LANES] = v_ref[r0:r0 + LANES, :].T

    def attend(units):
        nk = units[0][2].shape[0]
        off = 2 * KEYS_BACK - nk
        scores, biases = [], []
        for di, qv, kv, _, _ in units:
            kb = kv.astype(BF16)
            for h in range(2):
                qh = jnp.where(lo if h == 0 else jnp.logical_not(lo), qv, 0.0).astype(BF16)
                scores.append(lax.dot_general(qh, kb, _NT, preferred_element_type=F32))
                biases.append(b_ref[di, h, :, off:])
        s = jnp.concatenate(scores, axis=0) + jnp.concatenate(biases, axis=0)
        m = jnp.max(s, axis=-1, keepdims=True)
        p = jnp.exp(s - m)
        l = jnp.sum(p, axis=-1, keepdims=True)
        pn = (p * (1.0 / l)).astype(BF16)
        for g, (di, _, _, vv, out_rows) in enumerate(units):
            vb = vv.astype(BF16)
            r0 = 2 * g * KEYS_BACK
            r1 = r0 + KEYS_BACK
            o0 = jnp.dot(pn[r0:r1], vb, preferred_element_type=F32)
            o1 = jnp.dot(pn[r1:r1 + KEYS_BACK], vb, preferred_element_type=F32)
            od_ref[di, out_rows, :] = jnp.where(lo, o0, o1)
            m_ref[di, out_rows, :] = jnp.where(lo, m[r0:r1], m[r1:r1 + KEYS_BACK])
            l_ref[di, out_rows, :] = jnp.where(lo, l[r0:r1], l[r1:r1 + KEYS_BACK])

    def loop(n, group, make_unit):
        assert n % group == 0

        def step(t, carry):
            attend([make_unit(t * group + g) for g in range(group)])
            return carry
        lax.fori_loop(0, n // group, step, 0)

    blocks = seq // KEYS_BACK
    sub_blocks = per // KEYS_BACK

    def dense_unit(start):
        rows = pl.ds(start, KEYS_BACK)
        keys = rows if isinstance(start, int) and start == 0 else pl.ds(start - KEYS_BACK, 2 * KEYS_BACK)
        return 0, q_ref[rows, :] * B_SCALE, k_ref[keys, :], v_ref[keys, :], rows

    def mid_unit(start, first):
        rows = pl.ds(start, KEYS_BACK)
        keys = rows if first else pl.ds(start - KEYS_BACK, 2 * KEYS_BACK)
        return 1, qd_ref[rows, :], kd_ref[keys, :], vd_ref[keys, :], rows

    def wide_unit(r):
        rows = pl.ds((r & (RES - 1)) * per + (r >> _log2(RES)), KEYS_BACK, stride=RES)
        return 2, qd_ref[rows, :], kd_ref[rows, :], vd_ref[rows, :], rows

    attend([dense_unit(0)] + [mid_unit(c * per, True) for c in range(RES)])
    def group_of(n):
        return max(f for f in range(1, UNITS_IN_FLIGHT + 1) if n % f == 0)

    loop(blocks - 1, group_of(blocks - 1), lambda u: dense_unit(pl.multiple_of((u + 1) * KEYS_BACK, KEYS_BACK)))
    loop(RES * (sub_blocks - 1), group_of(RES * (sub_blocks - 1)),
         lambda u: mid_unit(pl.multiple_of((u & (RES - 1)) * per + (1 + (u >> _log2(RES))) * KEYS_BACK, KEYS_BACK),
                            False))
    loop(RES * RES, group_of(RES * RES), wide_unit)

    for c in range(RES):
        for j in range(sub_blocks):
            nat = pl.ds(c + RES * j * KEYS_BACK, KEYS_BACK, stride=RES)
            grp = pl.ds(c * per + j * KEYS_BACK, KEYS_BACK)
            ms = [m_ref[0, nat, :], m_ref[1, grp, :], m_ref[2, grp, :]]
            ls = [l_ref[0, nat, :], l_ref[1, grp, :], l_ref[2, grp, :]]
            outs = [od_ref[0, nat, :], od_ref[1, grp, :], od_ref[2, grp, :]]
            top = jnp.maximum(jnp.maximum(ms[0], ms[1]), ms[2])
            ws = [l * jnp.exp(m - top) for m, l in zip(ms, ls)]
            inv = 1.0 / (ws[0] + ws[1] + ws[2])
            o_ref[nat, :] = (ws[0] * inv) * outs[0] + (ws[1] * inv) * outs[1] + (ws[2] * inv) * outs[2]


def _dilated_attention_prompt(q, k, v, bias_tabs, batch, seq):
    assert DILATIONS == ((KEYS_BACK, 1), (KEYS_BACK * RES, RES), (KEYS_BACK * RES * RES, RES * RES))
    assert seq == KEYS_BACK * RES * RES
    pairs = B_WIDTH // LANES
    blk = pl.BlockSpec((seq, LANES), lambda b, p: (b, p))
    blk_t = pl.BlockSpec((LANES, seq), lambda b, p: (b * pairs + p, 0))
    out_t = jax.ShapeDtypeStruct((batch * B_WIDTH, seq), F32)
    return _call(functools.partial(_dilated_prompt_body, seq=seq), (batch, pairs),
                 [blk, blk, blk, pl.BlockSpec((N_DIL, 2, KEYS_BACK, 2 * KEYS_BACK), lambda b, p: (0, p, 0, 0))],
                 [blk, blk_t, blk_t], [jax.ShapeDtypeStruct((batch * seq, B_WIDTH), F32), out_t, out_t],
                 (q, k, v, bias_tabs), "dilated_attention_prompt",
                 scratch=[pltpu.VMEM((seq, LANES), F32)] * 3 + [pltpu.VMEM((N_DIL, seq, LANES), F32)] * 3)


def _sample_attn_body(q_ref, kn_ref, vn_ref, kt_ref, vt_ref, tf_ref, tc_ref, tn_ref, qm_ref, mk_ref, mv_ref,
                      yb_ref, ym_ref, *, ds, far):
    rows = B_HEADS * ds
    row = lax.broadcasted_iota(jnp.int32, (rows, B_WIDTH), 0)
    col = lax.broadcasted_iota(jnp.int32, (rows, B_WIDTH), 1)
    own = (row >> _log2(ds)) == (col >> _log2(B_HEAD_DIM))
    q = q_ref[...] * B_SCALE
    qbd = jnp.where(own, jnp.concatenate([q] * B_HEADS, axis=0), 0.0).astype(BF16)
    s_f = _mm(qbd, kt_ref[:, :far]) + tf_ref[...]
    s_c = _mm(qbd, kt_ref[:, far:])
    s_n = _mm(qbd, kn_ref[...], _NT)
    pc, pn, lses = [], [], []
    p_far = None
    for di in range(N_DIL):
        widest = di == N_DIL - 1
        sc = s_c + tc_ref[di]
        sn = s_n + tn_ref[di]
        m = jnp.maximum(jnp.max(sc, axis=-1, keepdims=True), jnp.max(sn, axis=-1, keepdims=True))
        if widest:
            m = jnp.maximum(m, jnp.max(s_f, axis=-1, keepdims=True))
        ec = jnp.exp(sc - m)
        en = jnp.exp(sn - m)
        l = jnp.sum(ec, axis=-1, keepdims=True) + jnp.sum(en, axis=-1, keepdims=True)
        if widest:
            ef = jnp.exp(s_f - m)
            l = l + jnp.sum(ef, axis=-1, keepdims=True)
        inv = 1.0 / l
        if widest:
            p_far = (ef * inv).astype(BF16)
        pc.append((ec * inv).astype(BF16))
        pn.append((en * inv).astype(BF16))
        lses.append(m + jnp.log(l))
    o = (_mm(jnp.concatenate(pc, axis=0), vt_ref[:, far:], _NT)
         + _mm(jnp.concatenate(pn, axis=0), vn_ref[...]))
    o_far = _mm(p_far, vt_ref[:, :far], _NT)
    top = jnp.maximum(jnp.maximum(lses[0], lses[1]), lses[2])
    ws = [jnp.exp(x - top) for x in lses]
    den = ws[0] + ws[1] + ws[2]
    mixed = (ws[N_DIL - 1] / den) * (o[(N_DIL - 1) * rows:] + o_far)
    for di in range(N_DIL - 1):
        mixed = mixed + (ws[di] / den) * o[di * rows:(di + 1) * rows]
    mixed = jnp.where(own, mixed, 0.0)
    yb = mixed[0:ds]
    for h in range(1, B_HEADS):
        yb = yb + mixed[h * ds:(h + 1) * ds]
    yb_ref[...] = yb
    _memory_attend_heads(qm_ref[...], mk_ref, mv_ref, ym_ref)


def _attention_sample(q, k_new, v_new, cache_k, cache_v, tabs, qm, cache_mk, cache_mv, ds):
    batch, n_cache = cache_k.shape[:2]
    rows = B_HEADS * ds
    far = n_cache - DILATIONS[-2][0]
    assert N_DIL == 3 and rows == LANES and far % LANES == 0
    assert all(w0 <= w1 for (w0, _), (w1, _) in zip(DILATIONS, DILATIONS[1:]))

    def keys_minor(x):
        return jnp.transpose(x, (0, 2, 3, 1)).reshape(batch * B_WIDTH, n_cache)

    new = pl.BlockSpec((ds, B_WIDTH), lambda b: (b, 0))
    cache = pl.BlockSpec((B_WIDTH, n_cache), lambda b: (b, 0))
    mem = pl.BlockSpec((N_MEM, M_HEADS, M_HEAD_DIM), lambda b: (b, 0, 0))
    cache_mk = cache_mk.reshape(batch * N_MEM, M_HEADS, M_HEAD_DIM)
    cache_mv = cache_mv.reshape(batch * N_MEM, M_HEADS, M_HEAD_DIM)
    qm_spec = pl.BlockSpec((ds, M_WIDTH), lambda b: (b, 0))
    return _call(functools.partial(_sample_attn_body, ds=ds, far=far), (batch,),
                 [new, new, new, cache, cache, _const_spec((rows, far)), _const_spec((N_DIL, rows, n_cache - far)),
                  _const_spec((N_DIL, rows, ds)), qm_spec, mem, mem],
                 [new, qm_spec],
                 [jax.ShapeDtypeStruct((batch * ds, B_WIDTH), F32), jax.ShapeDtypeStruct((batch * ds, M_WIDTH), F32)],
                 (q, k_new, v_new, keys_minor(cache_k), keys_minor(cache_v), tabs[N_DIL - 1, :, :far],
                  tabs[:, :, far:n_cache], tabs[:, :, n_cache:], qm, cache_mk, cache_mv),
                 "attention_sample")


def _out_router_body(ya_ref, yb_ref, ym_ref, x_ref, wo_ref, gf_ref, wr_ref, rec_ref):
    o_b = A_WIDTH
    o_m = A_WIDTH + B_WIDTH
    d = x_ref.shape[1]
    mix = (_mm(ya_ref[...], wo_ref[0:o_b, :]) + _mm(yb_ref[...], wo_ref[o_b:o_m, :])
           + _mm(ym_ref[...], wo_ref[o_m:, :]))
    h = x_ref[...] + mix
    rec_ref[:, :d] = h
    logits = _mm(_row_norm(h, gf_ref), wr_ref[...])
    lane_i = lax.broadcasted_iota(jnp.int32, logits.shape, 1)
    lane = lane_i.astype(F32)
    big = float(LANES)

    def first_argmax(vals):
        top = jnp.max(vals, axis=-1, keepdims=True)
        return top, jnp.min(jnp.where(vals == top, lane, big), axis=-1, keepdims=True)

    is_group = (lane_i >= N_EXPERTS) & (lane_i < N_EXPERTS + N_GROUPS)
    _, g_lane = first_argmax(jnp.where(is_group, logits, NEG_INF))
    grp = g_lane.astype(jnp.int32) - N_EXPERTS
    in_grp = (lane_i < N_EXPERTS) & ((lane_i >> _log2(EXPERTS_PER_GROUP)) == grp)
    e_log = jnp.where(in_grp, logits, NEG_INF)
    v1, i1 = first_argmax(e_log)
    v2, i2 = first_argmax(jnp.where(lane == i1, NEG_INF, e_log))
    e2 = jnp.exp(v2 - v1)
    den = 1.0 + e2
    rec_ref[:, d:] = (jnp.where(lane == i1, 1.0 / den, 0.0) + jnp.where(lane == i2, e2 / den, 0.0)
                      + jnp.where(lane_i == N_EXPERTS, grp.astype(F32), 0.0))


def _out_and_router(ya, yb, ym, x, w_out, g_ffn, w_router, tm, name):
    rows, d = x.shape

    def rs(width):
        return pl.BlockSpec((tm, width), lambda i: (i, 0))

    return _call(_out_router_body, (rows // tm,),
                 [rs(A_WIDTH), rs(B_WIDTH), rs(M_WIDTH), rs(d), _const_spec((d, d)), _const_spec((1, d)),
                  _const_spec((d, LANES))],
                 rs(d + LANES), jax.ShapeDtypeStruct((rows, d + LANES), F32),
                 (ya, yb, ym, x, w_out, g_ffn.reshape(1, d), w_router), name)


def _expert_ffn(n2, gates, expert_lane, wg_ref, wu_ref, wd_ref):
    hg = jnp.dot(n2, wg_ref[0], preferred_element_type=F32)
    hu = jnp.dot(n2, wu_ref[0], preferred_element_type=F32)
    lane = lax.broadcasted_iota(jnp.int32, gates.shape, 1)
    g = jnp.sum(jnp.where(lane == expert_lane, gates, 0.0), axis=-1, keepdims=True)
    act = hg * jax.nn.sigmoid(hg) * hu * g
    return jnp.dot(act.astype(BF16), wd_ref[0], preferred_element_type=F32)


def _moe_dense_body(rec_ref, gf_ref, wg_ref, wu_ref, wd_ref, o_ref, n2_ref):
    e = pl.program_id(1)
    d = o_ref.shape[1]

    @pl.when(e == 0)
    def _():
        h = rec_ref[:, :d]
        o_ref[...] = h
        n2_ref[...] = _row_norm(h, gf_ref)

    o_ref[...] += _expert_ffn(n2_ref[...], rec_ref[:, d:], e, wg_ref, wu_ref, wd_ref)


def _moe_dense(rec, g_ffn, w_gate, w_up, w_down, tm, name):
    rows = rec.shape[0]
    d, ff = w_gate.shape[1:]
    wspec = lambda a, b: pl.BlockSpec((1, a, b), lambda i, e: (e, 0, 0))
    return _call(_moe_dense_body, (rows // tm, N_EXPERTS),
                 [pl.BlockSpec((tm, d + LANES), lambda i, e: (i, 0)), _const_spec((1, d)),
                  wspec(d, ff), wspec(d, ff), wspec(ff, d)],
                 pl.BlockSpec((tm, d), lambda i, e: (i, 0)), jax.ShapeDtypeStruct((rows, d), F32),
                 (rec, g_ffn.reshape(1, d), w_gate, w_up, w_down), name,
                 semantics=("parallel", "arbitrary"), scratch=[pltpu.VMEM((tm, d), BF16)])


def _group_sort_plan(rec, tm):
    n, width = rec.shape
    d = width - LANES
    n_tiles = n // tm + N_GROUPS
    grp = rec[:, d + N_EXPERTS].astype(jnp.int32)
    order = jnp.argsort(grp, stable=True).astype(jnp.int32)
    counts = jnp.sum((grp[:, None] == jnp.arange(N_GROUPS)[None, :]).astype(jnp.int32), axis=0)
    tiles_per = (counts + tm - 1) // tm
    tile_end = jnp.cumsum(tiles_per)
    tile_start = tile_end - tiles_per
    tok_start = jnp.cumsum(counts) - counts
    n_used = tile_end[-1]
    t = jnp.arange(n_tiles, dtype=jnp.int32)
    tgrp = jnp.minimum(jnp.sum((t[:, None] >= tile_end[None, :]).astype(jnp.int32), axis=1), N_GROUPS - 1)
    row0 = (t - tile_start[tgrp]) * tm
    nvalid = jnp.where(t < n_used, jnp.clip(counts[tgrp] - row0, 0, tm), 0).astype(jnp.int32)
    k = row0[:, None] + jnp.arange(tm, dtype=jnp.int32)[None, :]
    real = jnp.arange(tm, dtype=jnp.int32)[None, :] < nvalid[:, None]
    src = jnp.where(real, order[jnp.clip(tok_start[tgrp][:, None] + k, 0, n - 1)], -1)
    return src.reshape(-1).astype(jnp.int32), tgrp.astype(jnp.int32), nvalid, n_used.reshape(1).astype(jnp.int32)


def _moe_grouped_body(src_ref, tgrp_ref, nvalid_ref, nused_ref, rec_hbm, gf_ref, wg_ref, wu_ref, wd_ref, y_hbm,
                      rec_buf, n2_ref, acc_ref, y_buf, gsem, ssem, *, tm):
    i = pl.program_id(0)
    e = pl.program_id(1)
    d = y_buf.shape[1]
    n_used = nused_ref[0]
    slot = i % 2

    def gather_rows(tile, slot_):
        def row(r, c):
            tok = jnp.maximum(src_ref[tile * tm + r], 0)
            pltpu.make_async_copy(rec_hbm.at[pl.ds(tok, 1)], rec_buf.at[slot_, pl.ds(r, 1)], gsem.at[slot_]).start()
            return c
        lax.fori_loop(0, tm, row, 0, unroll=8)

    def wait_gather(slot_):
        pltpu.make_async_copy(rec_hbm.at[pl.ds(0, tm)], rec_buf.at[slot_], gsem.at[slot_]).wait()

    def scatter_copy(tile, r):
        tok = src_ref[tile * tm + r]
        return pltpu.make_async_copy(y_buf.at[pl.ds(r, 1)], y_hbm.at[pl.ds(tok, 1)], ssem.at[0])

    def scatter_rows(tile, wait):
        def row(r, c):
            cp = scatter_copy(tile, r)
            if wait:
                cp.wait()
            else:
                cp.start()
            return c

        full = nvalid_ref[tile] == tm

        @pl.when(full)
        def _():
            if wait:
                pltpu.make_async_copy(y_buf, y_hbm.at[pl.ds(0, tm)], ssem.at[0]).wait()
            else:
                lax.fori_loop(0, tm, row, 0, unroll=8)

        @pl.when(jnp.logical_not(full))
        def _():
            lax.fori_loop(0, nvalid_ref[tile], row, 0)

    @pl.when(i < n_used)
    def _():
        @pl.when(e == 0)
        def _():
            @pl.when(i == 0)
            def _():
                gather_rows(0, 0)

            wait_gather(slot)

            @pl.when(i + 1 < n_used)
            def _():
                gather_rows(i + 1, 1 - slot)

            n2_ref[...] = _row_norm(rec_buf[slot, :, :d], gf_ref)
            acc_ref[...] = jnp.zeros_like(acc_ref)

        lane0 = tgrp_ref[i] * EXPERTS_PER_GROUP
        acc_ref[...] += _expert_ffn(n2_ref[...], rec_buf[slot, :, d:], lane0 + e, wg_ref, wu_ref, wd_ref)

        @pl.when(e == EXPERTS_PER_GROUP - 1)
        def _():
            @pl.when(i > 0)
            def _():
                scatter_rows(i - 1, True)

            y_buf[...] = rec_buf[slot, :, :d] + acc_ref[...]
            scatter_rows(i, False)

            @pl.when(i == n_used - 1)
            def _():
                scatter_rows(i, True)


def _moe_grouped(rec, g_ffn, w_gate, w_up, w_down, tm, name):
    n = rec.shape[0]
    d, ff = w_gate.shape[1:]
    src, tgrp, nvalid, n_used = _group_sort_plan(rec, tm)
    n_tiles = tgrp.shape[0]

    def widx(i, e, src_ref, tgrp_ref, nvalid_ref, nused_ref):
        last = nused_ref[0] - 1
        live = i <= last
        return (tgrp_ref[jnp.minimum(i, last)] * EXPERTS_PER_GROUP
                + jnp.where(live, e, EXPERTS_PER_GROUP - 1), 0, 0)

    scratch = [pltpu.VMEM((2, tm, d + LANES), F32), pltpu.VMEM((tm, d), BF16), pltpu.VMEM((tm, d), F32),
               pltpu.VMEM((tm, d), F32), pltpu.SemaphoreType.DMA((2,)), pltpu.SemaphoreType.DMA((1,))]
    blk = 3 * _nbytes((d, ff), BF16) + _nbytes((1, d), F32)
    scr = sum(_nbytes(s.shape, s.dtype) for s in scratch[:4])
    return pl.pallas_call(
        functools.partial(_moe_grouped_body, tm=tm),
        grid_spec=pltpu.PrefetchScalarGridSpec(
            num_scalar_prefetch=4,
            grid=(n_tiles, EXPERTS_PER_GROUP),
            in_specs=[pl.BlockSpec(memory_space=pl.ANY),
                      pl.BlockSpec((1, d), lambda i, e, *_: (0, 0)),
                      pl.BlockSpec((1, d, ff), widx), pl.BlockSpec((1, d, ff), widx), pl.BlockSpec((1, ff, d), widx)],
            out_specs=pl.BlockSpec(memory_space=pl.ANY),
            scratch_shapes=scratch),
        out_shape=jax.ShapeDtypeStruct((n, d), F32),
        compiler_params=pltpu.CompilerParams(dimension_semantics=("arbitrary", "arbitrary"),
                                             vmem_limit_bytes=_vmem_limit(blk, scr)),
        name=name,
    )(src, tgrp, nvalid, n_used, rec, g_ffn.reshape(1, d), w_gate, w_up, w_down)


def kernel(x_prompt, x_sample, mem_prompt, cache_win_k, cache_win_v, cache_mem_k, cache_mem_v, rel_bias, g_mix, w_in, g_av, w_s, b_s, g_qb, g_kb, g_qm, g_km, g_mem, w_mem_kv, w_out, g_ffn, w_router_group, w_router_expert, w_gate, w_up, w_down):
    batch, seq, d = x_prompt.shape
    dec_batch, ds, _ = x_sample.shape
    depth = w_in.shape[0]
    n_cache = cache_win_k.shape[2]
    assert seq % (KEYS_BACK * DILATIONS[-1][1]) == 0 and seq <= DILATIONS[-1][0] and seq % CHUNK == 0
    assert n_cache >= DILATIONS[-1][0] and ds <= CHUNK

    xp = x_prompt.reshape(batch * seq, d)
    xs = x_sample.reshape(dec_batch * ds, d)
    mem = mem_prompt.reshape(batch * N_MEM, d)
    rows_s = dec_batch * ds
    tm = _row_tile(batch * seq)

    bias_prompt = _prompt_bias_tables(rel_bias)
    bias_sample = _sample_bias_tables(rel_bias, n_cache, ds)
    causal = np.tril(np.ones((CHUNK, CHUNK), bool))
    same_row = (np.arange(rows_s)[:, None] // ds) == (np.arange(rows_s)[None, :] // ds)

    outs = [[] for _ in range(8)]
    for l in range(depth):
        w_in_b = w_in[l].astype(BF16)
        w_out_b = w_out[l].astype(BF16)
        w_router = jnp.concatenate(
            [jnp.transpose(w_router_expert[l], (1, 0, 2)).reshape(d, N_EXPERTS), w_router_group[l],
             jnp.zeros((d, LANES - N_EXPERTS - N_GROUPS), F32)], axis=1).astype(BF16)
        wg_b, wu_b, wd_b = w_gate[l].astype(BF16), w_up[l].astype(BF16), w_down[l].astype(BF16)
        ws_prompt = jnp.where(causal, w_s[l], 0.0)
        bs_prompt = b_s[l].T
        ws_small = jnp.where(causal[:ds, :ds], w_s[l][:, :ds, :ds], 0.0)
        ws_sample = jnp.where(same_row, jnp.tile(ws_small, (1, dec_batch, dec_batch)), 0.0)
        bs_sample = jnp.tile(b_s[l][:, :ds].T, (dec_batch, 1))

        mk, mv = _memory_kv(mem, g_mem[l], w_mem_kv[l].astype(BF16), g_km[l])
        ya, va, qb, kb, vb, ym = _mixer_inputs(xp, g_mix[l], w_in_b, g_av[l], ws_prompt, bs_prompt, g_qb[l],
                                               g_kb[l], g_qm[l], (mk, mv, seq), tm, "mixer_inputs_prompt")
        yb, kb_t, vb_t = _dilated_attention_prompt(qb, kb, vb, bias_prompt, batch, seq)
        rec = _out_and_router(ya, yb, ym, xp, w_out_b, g_ffn[l], w_router, tm // 2, "out_router_prompt")
        xp = _moe_grouped(rec, g_ffn[l], wg_b, wu_b, wd_b, tm, "moe_prompt")
        wp = min(DILATIONS[-1][0], seq)
        for out, x_t in ((outs[0], kb_t), (outs[1], vb_t)):
            x = jnp.transpose(x_t.reshape(batch, B_HEADS, B_HEAD_DIM, seq), (0, 3, 1, 2))
            out.append(x[:, seq - wp:])
        outs[2].append(va.reshape(batch, seq, A_WIDTH)[:, seq - CHUNK:].reshape(batch, CHUNK, A_GROUPS, A_DIM))
        outs[3].append(mk.reshape(batch, N_MEM, M_HEADS, M_HEAD_DIM))
        outs[4].append(mv.reshape(batch, N_MEM, M_HEADS, M_HEAD_DIM))

        ya, va, qb, kb, vb, qm = _mixer_inputs(xs, g_mix[l], w_in_b, g_av[l], ws_sample, bs_sample, g_qb[l],
                                               g_kb[l], g_qm[l], None, rows_s, "mixer_inputs_sample")
        yb, ym = _attention_sample(
            qb, kb, vb, cache_win_k[l], cache_win_v[l], bias_sample, qm, cache_mem_k[l], cache_mem_v[l], ds)
        rec = _out_and_router(ya, yb, ym, xs, w_out_b, g_ffn[l], w_router, rows_s, "out_router_sample")
        xs = _moe_dense(rec, g_ffn[l], wg_b, wu_b, wd_b, rows_s, "moe_sample")
        outs[5].append(kb.reshape(dec_batch, ds, B_HEADS, B_HEAD_DIM))
        outs[6].append(vb.reshape(dec_batch, ds, B_HEADS, B_HEAD_DIM))
        outs[7].append(va.reshape(dec_batch, ds, A_GROUPS, A_DIM))

    stacked = [jnp.stack(o, axis=0) for o in outs]
    return (xp.reshape(batch, seq, d), xs.reshape(dec_batch, ds, d), *stacked)
```

```python
import functools
import math

import numpy as np
import jax
import jax.numpy as jnp
from jax import lax
from jax.experimental import pallas as pl
from jax.experimental.pallas import tpu as pltpu

F32 = jnp.float32
BF16 = jnp.bfloat16

D_MODEL = 2048
A_GROUPS = 4
A_DIM = 128
A_WIDTH = A_GROUPS * A_DIM
CHUNK = 128
B_HEADS = 16
B_HEAD_DIM = 64
B_WIDTH = B_HEADS * B_HEAD_DIM
M_HEADS = 4
M_HEAD_DIM = 128
M_WIDTH = M_HEADS * M_HEAD_DIM
N_MEM = 256
DILATIONS = ((128, 1), (512, 4), (2048, 16))
N_DIL = len(DILATIONS)
KEYS_BACK = 128
RES = DILATIONS[1][1]
ROUTER_SUB_ROWS = 256
UNITS_IN_FLIGHT = 16
N_BUCKETS = 32
MAX_DISTANCE = 2048
N_GROUPS = 4
EXPERTS_PER_GROUP = 8
N_EXPERTS = N_GROUPS * EXPERTS_PER_GROUP
EPS = 1e-6

LANES = 128
V7X_VMEM_BYTES = 64 * 1024 * 1024
VMEM_CAP = V7X_VMEM_BYTES - 6 * 1024 * 1024
INV_SQRT2 = 0.7071067811865476
B_SCALE = B_HEAD_DIM ** -0.5
M_SCALE = M_HEAD_DIM ** -0.5
NEG_INF = float("-inf")

_NT = (((1,), (1,)), ((), ()))
_TN = (((0,), (0,)), ((), ()))


def _log2(n):
    assert n > 0 and n & (n - 1) == 0, n
    return n.bit_length() - 1


def _mm(a, b, dims=None):
    a, b = a.astype(BF16), b.astype(BF16)
    if dims is None:
        return jnp.dot(a, b, preferred_element_type=F32)
    return lax.dot_general(a, b, dims, preferred_element_type=F32)


def _vmem_limit(block_bytes, scratch_bytes=0):
    est = 2 * block_bytes + scratch_bytes + 16 * 1024 * 1024
    return int(min(max(est, 32 * 1024 * 1024), VMEM_CAP))


def _nbytes(shape, dtype):
    return int(np.prod(shape)) * jnp.dtype(dtype).itemsize


def _call(body, grid, in_specs, out_specs, out_shape, operands, name, semantics=None, scratch=()):
    outs = out_shape if isinstance(out_shape, (list, tuple)) else [out_shape]
    ospecs = out_specs if isinstance(out_specs, (list, tuple)) else [out_specs]
    blk = sum(_nbytes(s.block_shape, o.dtype) for s, o in zip(in_specs, operands))
    blk += sum(_nbytes(s.block_shape, o.dtype) for s, o in zip(ospecs, outs))
    scr = sum(_nbytes(s.shape, s.dtype) for s in scratch)
    if semantics is None:
        semantics = ("parallel",) * len(grid)
    return pl.pallas_call(
        body,
        grid=grid,
        in_specs=in_specs,
        out_specs=out_specs,
        out_shape=out_shape,
        scratch_shapes=list(scratch),
        compiler_params=pltpu.CompilerParams(dimension_semantics=semantics,
                                             vmem_limit_bytes=_vmem_limit(blk, scr)),
        name=name,
    )(*operands)


def _row_tile(rows):
    tile = 4 * LANES
    assert rows % tile == 0
    return tile


def _const_spec(shape):
    nd = len(shape)
    return pl.BlockSpec(shape, lambda *_: (0,) * nd, pipeline_mode=pl.Buffered(1))


def _row_norm(x, g_ref):
    return (x * lax.rsqrt(jnp.mean(x * x, axis=-1, keepdims=True) + EPS) * g_ref[...]).astype(BF16)


def _store_head_norm(z, g_ref, o_ref, head_dim):
    lane = lax.broadcasted_iota(jnp.int32, (z.shape[0], LANES), 1)
    for c0 in range(0, z.shape[1], LANES):
        blk = z[:, c0:c0 + LANES]
        sq = blk * blk
        if head_dim == LANES:
            inv = lax.rsqrt(jnp.mean(sq, axis=-1, keepdims=True) + EPS)
        else:
            lo = lane < head_dim
            s_lo = jnp.sum(jnp.where(lo, sq, 0.0), axis=-1, keepdims=True)
            s_hi = jnp.sum(jnp.where(lo, 0.0, sq), axis=-1, keepdims=True)
            inv = jnp.where(lo, lax.rsqrt(s_lo / head_dim + EPS), lax.rsqrt(s_hi / head_dim + EPS))
        o_ref[:, c0:c0 + LANES] = blk * inv * g_ref[:, c0:c0 + LANES]


def _memory_attend_heads(qm, mk_ref, mv_ref, ym_ref):
    for h in range(M_HEADS):
        c0 = h * M_HEAD_DIM
        if len(mk_ref.shape) == 3:
            mk, mv = mk_ref[:, h, :], mv_ref[:, h, :]
        else:
            mk, mv = mk_ref[:, c0:c0 + M_HEAD_DIM], mv_ref[:, c0:c0 + M_HEAD_DIM]
        s = _mm(qm[:, c0:c0 + M_HEAD_DIM], mk, _NT) * M_SCALE
        p = jnp.exp(s - jnp.max(s, axis=-1, keepdims=True))
        p = p / jnp.sum(p, axis=-1, keepdims=True)
        ym_ref[:, c0:c0 + M_HEAD_DIM] = _mm(p, mv)


def _mixer_inputs_body(*refs, cs, attend_memory):
    (x_ref, gmix_ref, w_ref, gav_ref, ws_ref, bs_ref, gq_ref, gk_ref, gqm_ref), refs = refs[:9], refs[9:]
    if attend_memory:
        (mk_ref, mv_ref), refs = refs[:2], refs[2:]
    ya_ref, va_ref, q_ref, k_ref, v_ref, m_ref, n_ref = refs[:7]
    o_b = 2 * A_WIDTH
    o_k = o_b + B_WIDTH
    o_v = o_k + B_WIDTH
    o_m = o_v + B_WIDTH
    n_ref[...] = _row_norm(x_ref[...], gmix_ref)
    tm = n_ref.shape[0]

    z = _mm(n_ref[...], w_ref[:, :o_b])
    z_q = _mm(n_ref[...], w_ref[:, o_b:o_k])
    uv = 0.5 * z * (1.0 + lax.erf(z * INV_SQRT2))
    for g in range(A_GROUPS):
        c0 = g * A_DIM
        u = uv[:, c0:c0 + A_DIM]
        v = uv[:, A_WIDTH + c0:A_WIDTH + c0 + A_DIM]
        va = v * lax.rsqrt(jnp.mean(v * v, axis=-1, keepdims=True) + EPS) * gav_ref[:, c0:c0 + A_DIM]
        va_ref[:, c0:c0 + A_DIM] = va
        for r0 in range(0, tm, cs):
            mixed = _mm(ws_ref[g], va[r0:r0 + cs]) + bs_ref[:, g:g + 1]
            ya_ref[r0:r0 + cs, c0:c0 + A_DIM] = u[r0:r0 + cs] * mixed

    z_k = _mm(n_ref[...], w_ref[:, o_k:o_v])
    _store_head_norm(z_q, gq_ref, q_ref, B_HEAD_DIM)
    v_ref[...] = _mm(n_ref[...], w_ref[:, o_v:o_m])
    _store_head_norm(z_k, gk_ref, k_ref, B_HEAD_DIM)
    z_m = _mm(n_ref[...], w_ref[:, o_m:])
    if attend_memory:
        qm_ref = refs[7]
        _store_head_norm(z_m, gqm_ref, qm_ref, M_HEAD_DIM)
        _memory_attend_heads(qm_ref[...], mk_ref, mv_ref, m_ref)
    else:
        _store_head_norm(z_m, gqm_ref, m_ref, M_HEAD_DIM)


def _mixer_inputs(x, g_mix, w_in, g_av, ws_masked, bs_col, g_qb, g_kb, g_qm, memory, tm, name):
    rows, d = x.shape
    cs = ws_masked.shape[1]

    def tiled(gain, heads):
        return jnp.tile(gain.reshape(1, -1), (1, heads))

    def rs(width):
        return pl.BlockSpec((tm, width), lambda i: (i, 0))

    in_specs = [rs(d), _const_spec((1, d)), _const_spec(w_in.shape), _const_spec((1, A_WIDTH)),
                _const_spec((A_GROUPS, cs, cs)), _const_spec((cs, A_GROUPS)), _const_spec((1, B_WIDTH)),
                _const_spec((1, B_WIDTH)), _const_spec((1, M_WIDTH))]
    operands = [x, g_mix.reshape(1, d), w_in, g_av.reshape(1, A_WIDTH), ws_masked, bs_col,
                tiled(g_qb, B_HEADS), tiled(g_kb, B_HEADS), tiled(g_qm, M_HEADS)]
    scratch = [pltpu.VMEM((tm, d), BF16)]
    if memory is not None:
        mk, mv, seq = memory
        mem = pl.BlockSpec((N_MEM, M_WIDTH), lambda i: (i // (seq // tm), 0))
        in_specs += [mem, mem]
        operands += [mk, mv]
        scratch.append(pltpu.VMEM((tm, M_WIDTH), F32))
    widths = (A_WIDTH, A_WIDTH, B_WIDTH, B_WIDTH, B_WIDTH, M_WIDTH)
    return _call(functools.partial(_mixer_inputs_body, cs=cs, attend_memory=memory is not None), (rows // tm,),
                 in_specs, [rs(w) for w in widths], [jax.ShapeDtypeStruct((rows, w), F32) for w in widths],
                 operands, name, scratch=scratch)


def _mem_kv_body(x_ref, gmem_ref, w_ref, g_ref, mk_ref, mv_ref):
    kv = _mm(_row_norm(x_ref[...], gmem_ref), w_ref[...])
    _store_head_norm(kv[:, :M_WIDTH], g_ref, mk_ref, M_HEAD_DIM)
    mv_ref[...] = kv[:, M_WIDTH:]


def _memory_kv(mem, g_mem, w_kv, g_km):
    rows, d = mem.shape
    out = jax.ShapeDtypeStruct((rows, M_WIDTH), F32)
    spec = pl.BlockSpec((N_MEM, M_WIDTH), lambda i: (i, 0))
    return _call(_mem_kv_body, (rows // N_MEM,),
                 [pl.BlockSpec((N_MEM, d), lambda i: (i, 0)), _const_spec((1, d)), _const_spec((d, 2 * M_WIDTH)),
                  _const_spec((1, M_WIDTH))],
                 [spec, spec], [out, out],
                 (mem, g_mem.reshape(1, d), w_kv, jnp.tile(g_km.reshape(1, M_HEAD_DIM), (1, M_HEADS))), "memory_kv")


def _t5_bucket_np(dist):
    max_exact = N_BUCKETS // 2
    d = np.maximum(dist, 0)
    df = np.maximum(d, 1).astype(np.float64)
    large = max_exact + (np.log(df / max_exact) / math.log(MAX_DISTANCE / max_exact)
                         * (N_BUCKETS - max_exact)).astype(np.int32)
    large = np.minimum(large, N_BUCKETS - 1)
    return np.where(d < max_exact, d, large)


def _bias_by_distance(rel_bias, dists):
    onehot = np.zeros((N_BUCKETS, len(dists)), np.float32)
    onehot[_t5_bucket_np(np.asarray(dists)), np.arange(len(dists))] = 1.0
    return jnp.dot(rel_bias.astype(F32).T, jnp.asarray(onehot), precision=lax.Precision.HIGHEST)


def _prompt_bias_tables(rel_bias):
    wrap = 2 * KEYS_BACK + 1
    tabs = []
    for _, dil in DILATIONS:
        by_t = _bias_by_distance(rel_bias, [(KEYS_BACK - t) * dil for t in range(KEYS_BACK + 1)])
        row = jnp.concatenate([by_t, jnp.full((B_HEADS, wrap - KEYS_BACK - 1), NEG_INF, F32)], axis=1)
        flat = jnp.tile(row, (1, KEYS_BACK))[:, :KEYS_BACK * 2 * KEYS_BACK]
        tabs.append(flat.reshape(B_HEADS, KEYS_BACK, 2 * KEYS_BACK))
    return jnp.stack(tabs, axis=0)


def _sample_bias_tables(rel_bias, n_cache, ds):
    n_keys = n_cache + ds
    dist = n_cache + np.arange(ds)[:, None] - np.arange(n_keys)[None, :]
    by_key = _bias_by_distance(rel_bias, [max(n_cache + ds - 1 - c, 0) for c in range(n_keys + ds - 1)])
    bias = jnp.stack([by_key[:, ds - 1 - i:ds - 1 - i + n_keys] for i in range(ds)], axis=1)
    bias = bias.reshape(B_HEADS * ds, n_keys)
    tabs = []
    for window, dil in DILATIONS:
        used = (dist >= 0) & (dist % dil == 0) & (dist <= window)
        used = np.broadcast_to(used[None], (B_HEADS, ds, n_keys)).reshape(B_HEADS * ds, n_keys)
        tabs.append(jnp.where(used, bias, NEG_INF))
    return jnp.stack(tabs, axis=0)


def _dilated_prompt_body(q_ref, k_ref, v_ref, b_ref, o_ref, kt_ref, vt_ref, qd_ref, kd_ref, vd_ref, od_ref, m_ref,
                         l_ref, *, seq):
    per = seq // RES
    lane = lax.broadcasted_iota(jnp.int32, (KEYS_BACK, LANES), 1)
    lo = lane < B_HEAD_DIM

    for c in range(RES):
        qd_ref[pl.ds(c * per, per), :] = q_ref[pl.ds(c, per, stride=RES), :] * B_SCALE
        kd_ref[pl.ds(c * per, per), :] = k_ref[pl.ds(c, per, stride=RES), :]
        vd_ref[pl.ds(c * per, per), :] = v_ref[pl.ds(c, per, stride=RES), :]

    for r0 in range(0, seq, LANES):
        kt_ref[:, r0:r0 + LANES] = k_ref[r0:r0 + LANES, :].T
        vt_ref[:, r0:r0 + LANES] = v_ref[r0:r0 + LANES, :].T

    def attend(units):
        nk = units[0][2].shape[0]
        off = 2 * KEYS_BACK - nk
        scores, biases = [], []
        for di, qv, kv, _, _ in units:
            kb = kv.astype(BF16)
            for h in range(2):
                qh = jnp.where(lo if h == 0 else jnp.logical_not(lo), qv, 0.0).astype(BF16)
                scores.append(lax.dot_general(qh, kb, _NT, preferred_element_type=F32))
                biases.append(b_ref[di, h, :, off:])
        s = jnp.concatenate(scores, axis=0) + jnp.concatenate(biases, axis=0)
        m = jnp.max(s, axis=-1, keepdims=True)
        p = jnp.exp(s - m)
        l = jnp.sum(p, axis=-1, keepdims=True)
        pn = (p * (1.0 / l)).astype(BF16)
        for g, (di, _, _, vv, out_rows) in enumerate(units):
            vb = vv.astype(BF16)
            r0 = 2 * g * KEYS_BACK
            r1 = r0 + KEYS_BACK
            o0 = jnp.dot(pn[r0:r1], vb, preferred_element_type=F32)
            o1 = jnp.dot(pn[r1:r1 + KEYS_BACK], vb, preferred_element_type=F32)
            od_ref[di, out_rows, :] = jnp.where(lo, o0, o1)
            m_ref[di, out_rows, :] = jnp.where(lo, m[r0:r1], m[r1:r1 + KEYS_BACK])
            l_ref[di, out_rows, :] = jnp.where(lo, l[r0:r1], l[r1:r1 + KEYS_BACK])

    def loop(n, group, make_unit):
        assert n % group == 0

        def step(t, carry):
            attend([make_unit(t * group + g) for g in range(group)])
            return carry
        lax.fori_loop(0, n // group, step, 0)

    blocks = seq // KEYS_BACK
    sub_blocks = per // KEYS_BACK

    def dense_unit(start):
        rows = pl.ds(start, KEYS_BACK)
        keys = rows if isinstance(start, int) and start == 0 else pl.ds(start - KEYS_BACK, 2 * KEYS_BACK)
        return 0, q_ref[rows, :] * B_SCALE, k_ref[keys, :], v_ref[keys, :], rows

    def mid_unit(start, first):
        rows = pl.ds(start, KEYS_BACK)
        keys = rows if first else pl.ds(start - KEYS_BACK, 2 * KEYS_BACK)
        return 1, qd_ref[rows, :], kd_ref[keys, :], vd_ref[keys, :], rows

    def wide_unit(r):
        rows = pl.ds((r & (RES - 1)) * per + (r >> _log2(RES)), KEYS_BACK, stride=RES)
        return 2, qd_ref[rows, :], kd_ref[rows, :], vd_ref[rows, :], rows

    attend([dense_unit(0)] + [mid_unit(c * per, True) for c in range(RES)])
    def group_of(n):
        return max(f for f in range(1, UNITS_IN_FLIGHT + 1) if n % f == 0)

    loop(blocks - 1, group_of(blocks - 1), lambda u: dense_unit(pl.multiple_of((u + 1) * KEYS_BACK, KEYS_BACK)))
    loop(RES * (sub_blocks - 1), group_of(RES * (sub_blocks - 1)),
         lambda u: mid_unit(pl.multiple_of((u & (RES - 1)) * per + (1 + (u >> _log2(RES))) * KEYS_BACK, KEYS_BACK),
                            False))
    loop(RES * RES, group_of(RES * RES), wide_unit)

    for c in range(RES):
        for j in range(sub_blocks):
            nat = pl.ds(c + RES * j * KEYS_BACK, KEYS_BACK, stride=RES)
            grp = pl.ds(c * per + j * KEYS_BACK, KEYS_BACK)
            ms = [m_ref[0, nat, :], m_ref[1, grp, :], m_ref[2, grp, :]]
            ls = [l_ref[0, nat, :], l_ref[1, grp, :], l_ref[2, grp, :]]
            outs = [od_ref[0, nat, :], od_ref[1, grp, :], od_ref[2, grp, :]]
            top = jnp.maximum(jnp.maximum(ms[0], ms[1]), ms[2])
            ws = [l * jnp.exp(m - top) for m, l in zip(ms, ls)]
            inv = 1.0 / (ws[0] + ws[1] + ws[2])
            o_ref[nat, :] = (ws[0] * inv) * outs[0] + (ws[1] * inv) * outs[1] + (ws[2] * inv) * outs[2]


def _dilated_attention_prompt(q, k, v, bias_tabs, batch, seq):
    assert DILATIONS == ((KEYS_BACK, 1), (KEYS_BACK * RES, RES), (KEYS_BACK * RES * RES, RES * RES))
    assert seq == KEYS_BACK * RES * RES
    pairs = B_WIDTH // LANES
    blk = pl.BlockSpec((seq, LANES), lambda b, p: (b, p))
    blk_t = pl.BlockSpec((LANES, seq), lambda b, p: (b * pairs + p, 0))
    out_t = jax.ShapeDtypeStruct((batch * B_WIDTH, seq), F32)
    return _call(functools.partial(_dilated_prompt_body, seq=seq), (batch, pairs),
                 [blk, blk, blk, pl.BlockSpec((N_DIL, 2, KEYS_BACK, 2 * KEYS_BACK), lambda b, p: (0, p, 0, 0))],
                 [blk, blk_t, blk_t], [jax.ShapeDtypeStruct((batch * seq, B_WIDTH), F32), out_t, out_t],
                 (q, k, v, bias_tabs), "dilated_attention_prompt",
                 scratch=[pltpu.VMEM((seq, LANES), F32)] * 3 + [pltpu.VMEM((N_DIL, seq, LANES), F32)] * 3)


def _sample_attn_body(q_ref, kn_ref, vn_ref, kt_ref, vt_ref, tf_ref, tc_ref, tn_ref, qm_ref, mk_ref, mv_ref,
                      yb_ref, ym_ref, *, ds, far):
    rows = B_HEADS * ds
    row = lax.broadcasted_iota(jnp.int32, (rows, B_WIDTH), 0)
    col = lax.broadcasted_iota(jnp.int32, (rows, B_WIDTH), 1)
    own = (row >> _log2(ds)) == (col >> _log2(B_HEAD_DIM))
    q = q_ref[...] * B_SCALE
    qbd = jnp.where(own, jnp.concatenate([q] * B_HEADS, axis=0), 0.0).astype(BF16)
    s_f = _mm(qbd, kt_ref[:, :far]) + tf_ref[...]
    s_c = _mm(qbd, kt_ref[:, far:])
    s_n = _mm(qbd, kn_ref[...], _NT)
    pc, pn, lses = [], [], []
    p_far = None
    for di in range(N_DIL):
        widest = di == N_DIL - 1
        sc = s_c + tc_ref[di]
        sn = s_n + tn_ref[di]
        m = jnp.maximum(jnp.max(sc, axis=-1, keepdims=True), jnp.max(sn, axis=-1, keepdims=True))
        if widest:
            m = jnp.maximum(m, jnp.max(s_f, axis=-1, keepdims=True))
        ec = jnp.exp(sc - m)
        en = jnp.exp(sn - m)
        l = jnp.sum(ec, axis=-1, keepdims=True) + jnp.sum(en, axis=-1, keepdims=True)
        if widest:
            ef = jnp.exp(s_f - m)
            l = l + jnp.sum(ef, axis=-1, keepdims=True)
        inv = 1.0 / l
        if widest:
            p_far = (ef * inv).astype(BF16)
        pc.append((ec * inv).astype(BF16))
        pn.append((en * inv).astype(BF16))
        lses.append(m + jnp.log(l))
    o = (_mm(jnp.concatenate(pc, axis=0), vt_ref[:, far:], _NT)
         + _mm(jnp.concatenate(pn, axis=0), vn_ref[...]))
    o_far = _mm(p_far, vt_ref[:, :far], _NT)
    top = jnp.maximum(jnp.maximum(lses[0], lses[1]), lses[2])
    ws = [jnp.exp(x - top) for x in lses]
    den = ws[0] + ws[1] + ws[2]
    mixed = (ws[N_DIL - 1] / den) * (o[(N_DIL - 1) * rows:] + o_far)
    for di in range(N_DIL - 1):
        mixed = mixed + (ws[di] / den) * o[di * rows:(di + 1) * rows]
    mixed = jnp.where(own, mixed, 0.0)
    yb = mixed[0:ds]
    for h in range(1, B_HEADS):
        yb = yb + mixed[h * ds:(h + 1) * ds]
    yb_ref[...] = yb
    _memory_attend_heads(qm_ref[...], mk_ref, mv_ref, ym_ref)


def _attention_sample(q, k_new, v_new, cache_k, cache_v, tabs, qm, cache_mk, cache_mv, ds):
    batch, n_cache = cache_k.shape[:2]
    rows = B_HEADS * ds
    far = n_cache - DILATIONS[-2][0]
    assert N_DIL == 3 and rows == LANES and far % LANES == 0
    assert all(w0 <= w1 for (w0, _), (w1, _) in zip(DILATIONS, DILATIONS[1:]))

    def keys_minor(x):
        return jnp.transpose(x, (0, 2, 3, 1)).reshape(batch * B_WIDTH, n_cache)

    new = pl.BlockSpec((ds, B_WIDTH), lambda b: (b, 0))
    cache = pl.BlockSpec((B_WIDTH, n_cache), lambda b: (b, 0))
    mem = pl.BlockSpec((N_MEM, M_HEADS, M_HEAD_DIM), lambda b: (b, 0, 0))
    cache_mk = cache_mk.reshape(batch * N_MEM, M_HEADS, M_HEAD_DIM)
    cache_mv = cache_mv.reshape(batch * N_MEM, M_HEADS, M_HEAD_DIM)
    qm_spec = pl.BlockSpec((ds, M_WIDTH), lambda b: (b, 0))
    return _call(functools.partial(_sample_attn_body, ds=ds, far=far), (batch,),
                 [new, new, new, cache, cache, _const_spec((rows, far)), _const_spec((N_DIL, rows, n_cache - far)),
                  _const_spec((N_DIL, rows, ds)), qm_spec, mem, mem],
                 [new, qm_spec],
                 [jax.ShapeDtypeStruct((batch * ds, B_WIDTH), F32), jax.ShapeDtypeStruct((batch * ds, M_WIDTH), F32)],
                 (q, k_new, v_new, keys_minor(cache_k), keys_minor(cache_v), tabs[N_DIL - 1, :, :far],
                  tabs[:, :, far:n_cache], tabs[:, :, n_cache:], qm, cache_mk, cache_mv),
                 "attention_sample")


def _out_router_body(ya_ref, yb_ref, ym_ref, x_ref, wo_ref, gf_ref, wr_ref, rec_ref):
    o_b = A_WIDTH
    o_m = A_WIDTH + B_WIDTH
    tm, d = x_ref.shape
    sub = min(tm, ROUTER_SUB_ROWS)
    lane_i = lax.broadcasted_iota(jnp.int32, (sub, LANES), 1)
    lane = lane_i.astype(F32)
    big = float(LANES)

    def first_argmax(vals):
        top = jnp.max(vals, axis=-1, keepdims=True)
        return top, jnp.min(jnp.where(vals == top, lane, big), axis=-1, keepdims=True)

    normed = []
    for r0 in range(0, tm, sub):
        rows = slice(r0, r0 + sub)
        mix = (_mm(ya_ref[rows, :], wo_ref[0:o_b, :]) + _mm(yb_ref[rows, :], wo_ref[o_b:o_m, :])
               + _mm(ym_ref[rows, :], wo_ref[o_m:, :]))
        h = x_ref[rows, :] + mix
        rec_ref[rows, :d] = h
        normed.append(_row_norm(h, gf_ref))
    for r0, n2 in zip(range(0, tm, sub), normed):
        logits = _mm(n2, wr_ref[...])
        is_group = (lane_i >= N_EXPERTS) & (lane_i < N_EXPERTS + N_GROUPS)
        _, g_lane = first_argmax(jnp.where(is_group, logits, NEG_INF))
        grp = g_lane.astype(jnp.int32) - N_EXPERTS
        in_grp = (lane_i < N_EXPERTS) & ((lane_i >> _log2(EXPERTS_PER_GROUP)) == grp)
        e_log = jnp.where(in_grp, logits, NEG_INF)
        v1, i1 = first_argmax(e_log)
        v2, i2 = first_argmax(jnp.where(lane == i1, NEG_INF, e_log))
        e2 = jnp.exp(v2 - v1)
        den = 1.0 + e2
        rec_ref[r0:r0 + sub, d:] = (jnp.where(lane == i1, 1.0 / den, 0.0) + jnp.where(lane == i2, e2 / den, 0.0)
                                    + jnp.where(lane_i == N_EXPERTS, grp.astype(F32), 0.0))


def _out_and_router(ya, yb, ym, x, w_out, g_ffn, w_router, tm, name):
    rows, d = x.shape

    def rs(width):
        return pl.BlockSpec((tm, width), lambda i: (i, 0))

    return _call(_out_router_body, (rows // tm,),
                 [rs(A_WIDTH), rs(B_WIDTH), rs(M_WIDTH), rs(d), _const_spec((d, d)), _const_spec((1, d)),
                  _const_spec((d, LANES))],
                 rs(d + LANES), jax.ShapeDtypeStruct((rows, d + LANES), F32),
                 (ya, yb, ym, x, w_out, g_ffn.reshape(1, d), w_router), name)


def _expert_ffn(n2, gates, expert_lane, wg_ref, wu_ref, wd_ref):
    hg = jnp.dot(n2, wg_ref[0], preferred_element_type=F32)
    hu = jnp.dot(n2, wu_ref[0], preferred_element_type=F32)
    lane = lax.broadcasted_iota(jnp.int32, gates.shape, 1)
    g = jnp.sum(jnp.where(lane == expert_lane, gates, 0.0), axis=-1, keepdims=True)
    act = hg * jax.nn.sigmoid(hg) * hu * g
    return jnp.dot(act.astype(BF16), wd_ref[0], preferred_element_type=F32)


def _moe_dense_body(rec_ref, gf_ref, wg_ref, wu_ref, wd_ref, o_ref, n2_ref):
    e = pl.program_id(1)
    d = o_ref.shape[1]

    @pl.when(e == 0)
    def _():
        h = rec_ref[:, :d]
        o_ref[...] = h
        n2_ref[...] = _row_norm(h, gf_ref)

    o_ref[...] += _expert_ffn(n2_ref[...], rec_ref[:, d:], e, wg_ref, wu_ref, wd_ref)


def _moe_dense(rec, g_ffn, w_gate, w_up, w_down, tm, name):
    rows = rec.shape[0]
    d, ff = w_gate.shape[1:]
    wspec = lambda a, b: pl.BlockSpec((1, a, b), lambda i, e: (e, 0, 0))
    return _call(_moe_dense_body, (rows // tm, N_EXPERTS),
                 [pl.BlockSpec((tm, d + LANES), lambda i, e: (i, 0)), _const_spec((1, d)),
                  wspec(d, ff), wspec(d, ff), wspec(ff, d)],
                 pl.BlockSpec((tm, d), lambda i, e: (i, 0)), jax.ShapeDtypeStruct((rows, d), F32),
                 (rec, g_ffn.reshape(1, d), w_gate, w_up, w_down), name,
                 semantics=("parallel", "arbitrary"), scratch=[pltpu.VMEM((tm, d), BF16)])


def _group_sort_plan(rec, tm):
    n, width = rec.shape
    d = width - LANES
    n_tiles = n // tm + N_GROUPS
    grp = rec[:, d + N_EXPERTS].astype(jnp.int32)
    order = jnp.argsort(grp, stable=True).astype(jnp.int32)
    counts = jnp.sum((grp[:, None] == jnp.arange(N_GROUPS)[None, :]).astype(jnp.int32), axis=0)
    tiles_per = (counts + tm - 1) // tm
    tile_end = jnp.cumsum(tiles_per)
    tile_start = tile_end - tiles_per
    tok_start = jnp.cumsum(counts) - counts
    n_used = tile_end[-1]
    t = jnp.arange(n_tiles, dtype=jnp.int32)
    tgrp = jnp.minimum(jnp.sum((t[:, None] >= tile_end[None, :]).astype(jnp.int32), axis=1), N_GROUPS - 1)
    row0 = (t - tile_start[tgrp]) * tm
    nvalid = jnp.where(t < n_used, jnp.clip(counts[tgrp] - row0, 0, tm), 0).astype(jnp.int32)
    k = row0[:, None] + jnp.arange(tm, dtype=jnp.int32)[None, :]
    real = jnp.arange(tm, dtype=jnp.int32)[None, :] < nvalid[:, None]
    src = jnp.where(real, order[jnp.clip(tok_start[tgrp][:, None] + k, 0, n - 1)], -1)
    return src.reshape(-1).astype(jnp.int32), tgrp.astype(jnp.int32), nvalid, n_used.reshape(1).astype(jnp.int32)


def _moe_grouped_body(src_ref, tgrp_ref, nvalid_ref, nused_ref, rec_hbm, gf_ref, wg_ref, wu_ref, wd_ref, y_hbm,
                      rec_buf, n2_ref, acc_ref, y_buf, gsem, ssem, *, tm):
    i = pl.program_id(0)
    e = pl.program_id(1)
    d = y_buf.shape[1]
    n_used = nused_ref[0]
    slot = i % 2

    def gather_row(tile, slot_, r):
        tok = jnp.maximum(src_ref[tile * tm + r], 0)
        pltpu.make_async_copy(rec_hbm.at[pl.ds(tok, 1)], rec_buf.at[slot_, pl.ds(r, 1)], gsem.at[slot_]).start()

    def wait_gather(slot_):
        pltpu.make_async_copy(rec_hbm.at[pl.ds(0, tm)], rec_buf.at[slot_], gsem.at[slot_]).wait()

    def scatter_copy(tile, r):
        tok = src_ref[tile * tm + r]
        return pltpu.make_async_copy(y_buf.at[pl.ds(r, 1)], y_hbm.at[pl.ds(tok, 1)], ssem.at[0])

    def scatter_rows(tile, wait):
        def row(r, c):
            cp = scatter_copy(tile, r)
            if wait:
                cp.wait()
            else:
                cp.start()
            return c

        full = nvalid_ref[tile] == tm

        @pl.when(full)
        def _():
            if wait:
                pltpu.make_async_copy(y_buf, y_hbm.at[pl.ds(0, tm)], ssem.at[0]).wait()
            else:
                lax.fori_loop(0, tm, row, 0, unroll=8)

        @pl.when(jnp.logical_not(full))
        def _():
            lax.fori_loop(0, nvalid_ref[tile], row, 0)

    used = i < n_used
    last_tile = pl.num_programs(0) - 1
    per_step = tm // EXPERTS_PER_GROUP

    def prefetch_share():
        nxt = jnp.minimum(i + 1, last_tile)
        for j in range(per_step):
            gather_row(nxt, 1 - slot, e * per_step + j)

    @pl.when(e == 0)
    def _():
        @pl.when(i == 0)
        def _():
            def first_tile_row(r, c):
                gather_row(0, 0, r)
                return c
            lax.fori_loop(0, tm, first_tile_row, 0, unroll=8)

        wait_gather(slot)

        @pl.when(used)
        def _():
            n2_ref[...] = _row_norm(rec_buf[slot, :, :d], gf_ref)
            acc_ref[...] = jnp.zeros_like(acc_ref)

    @pl.when(jnp.logical_not(used))
    def _():
        prefetch_share()

    @pl.when(used)
    def _():
        prefetch_share()
        lane0 = tgrp_ref[i] * EXPERTS_PER_GROUP
        acc_ref[...] += _expert_ffn(n2_ref[...], rec_buf[slot, :, d:], lane0 + e, wg_ref, wu_ref, wd_ref)

        @pl.when(e == EXPERTS_PER_GROUP - 1)
        def _():
            @pl.when(i > 0)
            def _():
                scatter_rows(i - 1, True)

            y_buf[...] = rec_buf[slot, :, :d] + acc_ref[...]
            scatter_rows(i, False)

            @pl.when(i == n_used - 1)
            def _():
                scatter_rows(i, True)

    @pl.when((i == last_tile) & (e == EXPERTS_PER_GROUP - 1))
    def _():
        wait_gather(1 - slot)


def _moe_grouped(rec, g_ffn, w_gate, w_up, w_down, tm, name):
    n = rec.shape[0]
    d, ff = w_gate.shape[1:]
    src, tgrp, nvalid, n_used = _group_sort_plan(rec, tm)
    n_tiles = tgrp.shape[0]

    def widx(i, e, src_ref, tgrp_ref, nvalid_ref, nused_ref):
        last = nused_ref[0] - 1
        live = i <= last
        return (tgrp_ref[jnp.minimum(i, last)] * EXPERTS_PER_GROUP
                + jnp.where(live, e, EXPERTS_PER_GROUP - 1), 0, 0)

    scratch = [pltpu.VMEM((2, tm, d + LANES), F32), pltpu.VMEM((tm, d), BF16), pltpu.VMEM((tm, d), F32),
               pltpu.VMEM((tm, d), F32), pltpu.SemaphoreType.DMA((2,)), pltpu.SemaphoreType.DMA((1,))]
    blk = 3 * _nbytes((d, ff), BF16) + _nbytes((1, d), F32)
    scr = sum(_nbytes(s.shape, s.dtype) for s in scratch[:4])
    return pl.pallas_call(
        functools.partial(_moe_grouped_body, tm=tm),
        grid_spec=pltpu.PrefetchScalarGridSpec(
            num_scalar_prefetch=4,
            grid=(n_tiles, EXPERTS_PER_GROUP),
            in_specs=[pl.BlockSpec(memory_space=pl.ANY),
                      pl.BlockSpec((1, d), lambda i, e, *_: (0, 0)),
                      pl.BlockSpec((1, d, ff), widx), pl.BlockSpec((1, d, ff), widx), pl.BlockSpec((1, ff, d), widx)],
            out_specs=pl.BlockSpec(memory_space=pl.ANY),
            scratch_shapes=scratch),
        out_shape=jax.ShapeDtypeStruct((n, d), F32),
        compiler_params=pltpu.CompilerParams(dimension_semantics=("arbitrary", "arbitrary"),
                                             vmem_limit_bytes=_vmem_limit(blk, scr)),
        name=name,
    )(src, tgrp, nvalid, n_used, rec, g_ffn.reshape(1, d), w_gate, w_up, w_down)


def kernel(x_prompt, x_sample, mem_prompt, cache_win_k, cache_win_v, cache_mem_k, cache_mem_v, rel_bias, g_mix, w_in, g_av, w_s, b_s, g_qb, g_kb, g_qm, g_km, g_mem, w_mem_kv, w_out, g_ffn, w_router_group, w_router_expert, w_gate, w_up, w_down):
    batch, seq, d = x_prompt.shape
    dec_batch, ds, _ = x_sample.shape
    depth = w_in.shape[0]
    n_cache = cache_win_k.shape[2]
    assert seq % (KEYS_BACK * DILATIONS[-1][1]) == 0 and seq <= DILATIONS[-1][0] and seq % CHUNK == 0
    assert n_cache >= DILATIONS[-1][0] and ds <= CHUNK

    xp = x_prompt.reshape(batch * seq, d)
    xs = x_sample.reshape(dec_batch * ds, d)
    mem = mem_prompt.reshape(batch * N_MEM, d)
    rows_s = dec_batch * ds
    tm = _row_tile(batch * seq)

    bias_prompt = _prompt_bias_tables(rel_bias)
    bias_sample = _sample_bias_tables(rel_bias, n_cache, ds)
    causal = np.tril(np.ones((CHUNK, CHUNK), bool))
    same_row = (np.arange(rows_s)[:, None] // ds) == (np.arange(rows_s)[None, :] // ds)

    outs = [[] for _ in range(8)]
    for l in range(depth):
        w_in_b = w_in[l].astype(BF16)
        w_out_b = w_out[l].astype(BF16)
        w_router = jnp.concatenate(
            [jnp.transpose(w_router_expert[l], (1, 0, 2)).reshape(d, N_EXPERTS), w_router_group[l],
             jnp.zeros((d, LANES - N_EXPERTS - N_GROUPS), F32)], axis=1).astype(BF16)
        wg_b, wu_b, wd_b = w_gate[l].astype(BF16), w_up[l].astype(BF16), w_down[l].astype(BF16)
        ws_prompt = jnp.where(causal, w_s[l], 0.0)
        bs_prompt = b_s[l].T
        ws_small = jnp.where(causal[:ds, :ds], w_s[l][:, :ds, :ds], 0.0)
        ws_sample = jnp.where(same_row, jnp.tile(ws_small, (1, dec_batch, dec_batch)), 0.0)
        bs_sample = jnp.tile(b_s[l][:, :ds].T, (dec_batch, 1))

        mk, mv = _memory_kv(mem, g_mem[l], w_mem_kv[l].astype(BF16), g_km[l])
        ya, va, qb, kb, vb, ym = _mixer_inputs(xp, g_mix[l], w_in_b, g_av[l], ws_prompt, bs_prompt, g_qb[l],
                                               g_kb[l], g_qm[l], (mk, mv, seq), tm, "mixer_inputs_prompt")
        yb, kb_t, vb_t = _dilated_attention_prompt(qb, kb, vb, bias_prompt, batch, seq)
        rec = _out_and_router(ya, yb, ym, xp, w_out_b, g_ffn[l], w_router, tm, "out_router_prompt")
        xp = _moe_grouped(rec, g_ffn[l], wg_b, wu_b, wd_b, tm, "moe_prompt")
        wp = min(DILATIONS[-1][0], seq)
        for out, x_t in ((outs[0], kb_t), (outs[1], vb_t)):
            x = jnp.transpose(x_t.reshape(batch, B_HEADS, B_HEAD_DIM, seq), (0, 3, 1, 2))
            out.append(x[:, seq - wp:])
        outs[2].append(va.reshape(batch, seq, A_WIDTH)[:, seq - CHUNK:].reshape(batch, CHUNK, A_GROUPS, A_DIM))
        outs[3].append(mk.reshape(batch, N_MEM, M_HEADS, M_HEAD_DIM))
        outs[4].append(mv.reshape(batch, N_MEM, M_HEADS, M_HEAD_DIM))

        ya, va, qb, kb, vb, qm = _mixer_inputs(xs, g_mix[l], w_in_b, g_av[l], ws_sample, bs_sample, g_qb[l],
                                               g_kb[l], g_qm[l], None, rows_s, "mixer_inputs_sample")
        yb, ym = _attention_sample(
            qb, kb, vb, cache_win_k[l], cache_win_v[l], bias_sample, qm, cache_mem_k[l], cache_mem_v[l], ds)
        rec = _out_and_router(ya, yb, ym, xs, w_out_b, g_ffn[l], w_router, rows_s, "out_router_sample")
        xs = _moe_dense(rec, g_ffn[l], wg_b, wu_b, wd_b, rows_s, "moe_sample")
        outs[5].append(kb.reshape(dec_batch, ds, B_HEADS, B_HEAD_DIM))
        outs[6].append(vb.reshape(dec_batch, ds, B_HEADS, B_HEAD_DIM))
        outs[7].append(va.reshape(dec_batch, ds, A_GROUPS, A_DIM))

    stacked = [jnp.stack(o, axis=0) for o in outs]
    return (xp.reshape(batch, seq, d), xs.reshape(dec_batch, ds, d), *stacked)
```

```python
import functools
import math

import numpy as np
import jax
import jax.numpy as jnp
from jax import lax
from jax.experimental import pallas as pl
from jax.experimental.pallas import tpu as pltpu

F32 = jnp.float32
BF16 = jnp.bfloat16

D_MODEL = 2048
A_GROUPS = 4
A_DIM = 128
A_WIDTH = A_GROUPS * A_DIM
CHUNK = 128
B_HEADS = 16
B_HEAD_DIM = 64
B_WIDTH = B_HEADS * B_HEAD_DIM
M_HEADS = 4
M_HEAD_DIM = 128
M_WIDTH = M_HEADS * M_HEAD_DIM
N_MEM = 256
DILATIONS = ((128, 1), (512, 4), (2048, 16))
N_DIL = len(DILATIONS)
KEYS_BACK = 128
RES = DILATIONS[1][1]
DMA_ROWS_PER_TRIP = 16
ROUTER_SUB_ROWS = 256
UNITS_IN_FLIGHT = 16
N_BUCKETS = 32
MAX_DISTANCE = 2048
N_GROUPS = 4
EXPERTS_PER_GROUP = 8
N_EXPERTS = N_GROUPS * EXPERTS_PER_GROUP
EPS = 1e-6

LANES = 128
V7X_VMEM_BYTES = 64 * 1024 * 1024
VMEM_CAP = V7X_VMEM_BYTES - 6 * 1024 * 1024
INV_SQRT2 = 0.7071067811865476
B_SCALE = B_HEAD_DIM ** -0.5
M_SCALE = M_HEAD_DIM ** -0.5
NEG_INF = float("-inf")

_NT = (((1,), (1,)), ((), ()))
_TN = (((0,), (0,)), ((), ()))


def _log2(n):
    assert n > 0 and n & (n - 1) == 0, n
    return n.bit_length() - 1


def _mm(a, b, dims=None):
    a, b = a.astype(BF16), b.astype(BF16)
    if dims is None:
        return jnp.dot(a, b, preferred_element_type=F32)
    return lax.dot_general(a, b, dims, preferred_element_type=F32)


def _vmem_limit(block_bytes, scratch_bytes=0):
    est = 2 * block_bytes + scratch_bytes + 16 * 1024 * 1024
    return int(min(max(est, 32 * 1024 * 1024), VMEM_CAP))


def _nbytes(shape, dtype):
    return int(np.prod(shape)) * jnp.dtype(dtype).itemsize


def _call(body, grid, in_specs, out_specs, out_shape, operands, name, semantics=None, scratch=()):
    outs = out_shape if isinstance(out_shape, (list, tuple)) else [out_shape]
    ospecs = out_specs if isinstance(out_specs, (list, tuple)) else [out_specs]
    blk = sum(_nbytes(s.block_shape, o.dtype) for s, o in zip(in_specs, operands))
    blk += sum(_nbytes(s.block_shape, o.dtype) for s, o in zip(ospecs, outs))
    scr = sum(_nbytes(s.shape, s.dtype) for s in scratch)
    if semantics is None:
        semantics = ("parallel",) * len(grid)
    return pl.pallas_call(
        body,
        grid=grid,
        in_specs=in_specs,
        out_specs=out_specs,
        out_shape=out_shape,
        scratch_shapes=list(scratch),
        compiler_params=pltpu.CompilerParams(dimension_semantics=semantics,
                                             vmem_limit_bytes=_vmem_limit(blk, scr)),
        name=name,
    )(*operands)


def _row_tile(rows):
    tile = 4 * LANES
    assert rows % tile == 0
    return tile


def _const_spec(shape):
    nd = len(shape)
    return pl.BlockSpec(shape, lambda *_: (0,) * nd, pipeline_mode=pl.Buffered(1))


def _row_norm(x, g_ref):
    return (x * lax.rsqrt(jnp.mean(x * x, axis=-1, keepdims=True) + EPS) * g_ref[...]).astype(BF16)


def _store_head_norm(z, g_ref, o_ref, head_dim):
    lane = lax.broadcasted_iota(jnp.int32, (z.shape[0], LANES), 1)
    for c0 in range(0, z.shape[1], LANES):
        blk = z[:, c0:c0 + LANES]
        sq = blk * blk
        if head_dim == LANES:
            inv = lax.rsqrt(jnp.mean(sq, axis=-1, keepdims=True) + EPS)
        else:
            lo = lane < head_dim
            s_lo = jnp.sum(jnp.where(lo, sq, 0.0), axis=-1, keepdims=True)
            s_hi = jnp.sum(jnp.where(lo, 0.0, sq), axis=-1, keepdims=True)
            inv = jnp.where(lo, lax.rsqrt(s_lo / head_dim + EPS), lax.rsqrt(s_hi / head_dim + EPS))
        o_ref[:, c0:c0 + LANES] = blk * inv * g_ref[:, c0:c0 + LANES]


def _memory_attend_heads(qm, mk_ref, mv_ref, ym_ref):
    for h in range(M_HEADS):
        c0 = h * M_HEAD_DIM
        if len(mk_ref.shape) == 3:
            mk, mv = mk_ref[:, h, :], mv_ref[:, h, :]
        else:
            mk, mv = mk_ref[:, c0:c0 + M_HEAD_DIM], mv_ref[:, c0:c0 + M_HEAD_DIM]
        s = _mm(qm[:, c0:c0 + M_HEAD_DIM], mk, _NT) * M_SCALE
        p = jnp.exp(s - jnp.max(s, axis=-1, keepdims=True))
        p = p / jnp.sum(p, axis=-1, keepdims=True)
        ym_ref[:, c0:c0 + M_HEAD_DIM] = _mm(p, mv)


def _mixer_inputs_body(*refs, cs, attend_memory):
    (x_ref, gmix_ref, w_ref, gav_ref, ws_ref, bs_ref, gq_ref, gk_ref, gqm_ref), refs = refs[:9], refs[9:]
    if attend_memory:
        (mk_ref, mv_ref), refs = refs[:2], refs[2:]
    ya_ref, va_ref, q_ref, k_ref, v_ref, m_ref, n_ref = refs[:7]
    o_b = 2 * A_WIDTH
    o_k = o_b + B_WIDTH
    o_v = o_k + B_WIDTH
    o_m = o_v + B_WIDTH
    n_ref[...] = _row_norm(x_ref[...], gmix_ref)
    tm = n_ref.shape[0]

    z = _mm(n_ref[...], w_ref[:, :o_b])
    z_q = _mm(n_ref[...], w_ref[:, o_b:o_k])
    uv = 0.5 * z * (1.0 + lax.erf(z * INV_SQRT2))
    for g in range(A_GROUPS):
        c0 = g * A_DIM
        u = uv[:, c0:c0 + A_DIM]
        v = uv[:, A_WIDTH + c0:A_WIDTH + c0 + A_DIM]
        va = v * lax.rsqrt(jnp.mean(v * v, axis=-1, keepdims=True) + EPS) * gav_ref[:, c0:c0 + A_DIM]
        va_ref[:, c0:c0 + A_DIM] = va
        for r0 in range(0, tm, cs):
            mixed = _mm(ws_ref[g], va[r0:r0 + cs]) + bs_ref[:, g:g + 1]
            ya_ref[r0:r0 + cs, c0:c0 + A_DIM] = u[r0:r0 + cs] * mixed

    z_k = _mm(n_ref[...], w_ref[:, o_k:o_v])
    _store_head_norm(z_q, gq_ref, q_ref, B_HEAD_DIM)
    v_ref[...] = _mm(n_ref[...], w_ref[:, o_v:o_m])
    _store_head_norm(z_k, gk_ref, k_ref, B_HEAD_DIM)
    z_m = _mm(n_ref[...], w_ref[:, o_m:])
    if attend_memory:
        qm_ref = refs[7]
        _store_head_norm(z_m, gqm_ref, qm_ref, M_HEAD_DIM)
        _memory_attend_heads(qm_ref[...], mk_ref, mv_ref, m_ref)
    else:
        _store_head_norm(z_m, gqm_ref, m_ref, M_HEAD_DIM)


def _mixer_inputs(x, g_mix, w_in, g_av, ws_masked, bs_col, g_qb, g_kb, g_qm, memory, tm, name):
    rows, d = x.shape
    cs = ws_masked.shape[1]

    def tiled(gain, heads):
        return jnp.tile(gain.reshape(1, -1), (1, heads))

    def rs(width):
        return pl.BlockSpec((tm, width), lambda i: (i, 0))

    in_specs = [rs(d), _const_spec((1, d)), _const_spec(w_in.shape), _const_spec((1, A_WIDTH)),
                _const_spec((A_GROUPS, cs, cs)), _const_spec((cs, A_GROUPS)), _const_spec((1, B_WIDTH)),
                _const_spec((1, B_WIDTH)), _const_spec((1, M_WIDTH))]
    operands = [x, g_mix.reshape(1, d), w_in, g_av.reshape(1, A_WIDTH), ws_masked, bs_col,
                tiled(g_qb, B_HEADS), tiled(g_kb, B_HEADS), tiled(g_qm, M_HEADS)]
    scratch = [pltpu.VMEM((tm, d), BF16)]
    if memory is not None:
        mk, mv, seq = memory
        mem = pl.BlockSpec((N_MEM, M_WIDTH), lambda i: (i // (seq // tm), 0))
        in_specs += [mem, mem]
        operands += [mk, mv]
        scratch.append(pltpu.VMEM((tm, M_WIDTH), F32))
    widths = (A_WIDTH, A_WIDTH, B_WIDTH, B_WIDTH, B_WIDTH, M_WIDTH)
    return _call(functools.partial(_mixer_inputs_body, cs=cs, attend_memory=memory is not None), (rows // tm,),
                 in_specs, [rs(w) for w in widths], [jax.ShapeDtypeStruct((rows, w), F32) for w in widths],
                 operands, name, scratch=scratch)


def _mem_kv_body(x_ref, gmem_ref, w_ref, g_ref, mk_ref, mv_ref):
    kv = _mm(_row_norm(x_ref[...], gmem_ref), w_ref[...])
    _store_head_norm(kv[:, :M_WIDTH], g_ref, mk_ref, M_HEAD_DIM)
    mv_ref[...] = kv[:, M_WIDTH:]


def _memory_kv(mem, g_mem, w_kv, g_km):
    rows, d = mem.shape
    out = jax.ShapeDtypeStruct((rows, M_WIDTH), F32)
    spec = pl.BlockSpec((N_MEM, M_WIDTH), lambda i: (i, 0))
    return _call(_mem_kv_body, (rows // N_MEM,),
                 [pl.BlockSpec((N_MEM, d), lambda i: (i, 0)), _const_spec((1, d)), _const_spec((d, 2 * M_WIDTH)),
                  _const_spec((1, M_WIDTH))],
                 [spec, spec], [out, out],
                 (mem, g_mem.reshape(1, d), w_kv, jnp.tile(g_km.reshape(1, M_HEAD_DIM), (1, M_HEADS))), "memory_kv")


def _t5_bucket_np(dist):
    max_exact = N_BUCKETS // 2
    d = np.maximum(dist, 0)
    df = np.maximum(d, 1).astype(np.float64)
    large = max_exact + (np.log(df / max_exact) / math.log(MAX_DISTANCE / max_exact)
                         * (N_BUCKETS - max_exact)).astype(np.int32)
    large = np.minimum(large, N_BUCKETS - 1)
    return np.where(d < max_exact, d, large)


def _bias_by_distance(rel_bias, dists):
    onehot = np.zeros((N_BUCKETS, len(dists)), np.float32)
    onehot[_t5_bucket_np(np.asarray(dists)), np.arange(len(dists))] = 1.0
    return jnp.dot(rel_bias.astype(F32).T, jnp.asarray(onehot), precision=lax.Precision.HIGHEST)


def _prompt_bias_tables(rel_bias):
    wrap = 2 * KEYS_BACK + 1
    tabs = []
    for _, dil in DILATIONS:
        by_t = _bias_by_distance(rel_bias, [(KEYS_BACK - t) * dil for t in range(KEYS_BACK + 1)])
        row = jnp.concatenate([by_t, jnp.full((B_HEADS, wrap - KEYS_BACK - 1), NEG_INF, F32)], axis=1)
        flat = jnp.tile(row, (1, KEYS_BACK))[:, :KEYS_BACK * 2 * KEYS_BACK]
        tabs.append(flat.reshape(B_HEADS, KEYS_BACK, 2 * KEYS_BACK))
    return jnp.stack(tabs, axis=0)


def _sample_bias_tables(rel_bias, n_cache, ds):
    n_keys = n_cache + ds
    dist = n_cache + np.arange(ds)[:, None] - np.arange(n_keys)[None, :]
    by_key = _bias_by_distance(rel_bias, [max(n_cache + ds - 1 - c, 0) for c in range(n_keys + ds - 1)])
    bias = jnp.stack([by_key[:, ds - 1 - i:ds - 1 - i + n_keys] for i in range(ds)], axis=1)
    bias = bias.reshape(B_HEADS * ds, n_keys)
    tabs = []
    for window, dil in DILATIONS:
        used = (dist >= 0) & (dist % dil == 0) & (dist <= window)
        used = np.broadcast_to(used[None], (B_HEADS, ds, n_keys)).reshape(B_HEADS * ds, n_keys)
        tabs.append(jnp.where(used, bias, NEG_INF))
    return jnp.stack(tabs, axis=0)


def _dilated_prompt_body(q_ref, k_ref, v_ref, b_ref, o_ref, kt_ref, vt_ref, qd_ref, kd_ref, vd_ref, od_ref, m_ref,
                         l_ref, *, seq):
    per = seq // RES
    lane = lax.broadcasted_iota(jnp.int32, (KEYS_BACK, LANES), 1)
    lo = lane < B_HEAD_DIM

    for c in range(RES):
        qd_ref[pl.ds(c * per, per), :] = q_ref[pl.ds(c, per, stride=RES), :] * B_SCALE
        kd_ref[pl.ds(c * per, per), :] = k_ref[pl.ds(c, per, stride=RES), :]
        vd_ref[pl.ds(c * per, per), :] = v_ref[pl.ds(c, per, stride=RES), :]

    for r0 in range(0, seq, LANES):
        kt_ref[:, r0:r0 + LANES] = k_ref[r0:r0 + LANES, :].T
        vt_ref[:, r0:r0 + LANES] = v_ref[r0:r0 + LANES, :].T

    def attend(units):
        nk = units[0][2].shape[0]
        off = 2 * KEYS_BACK - nk
        scores, biases = [], []
        for di, qv, kv, _, _ in units:
            kb = kv.astype(BF16)
            for h in range(2):
                qh = jnp.where(lo if h == 0 else jnp.logical_not(lo), qv, 0.0).astype(BF16)
                scores.append(lax.dot_general(qh, kb, _NT, preferred_element_type=F32))
                biases.append(b_ref[di, h, :, off:])
        s = jnp.concatenate(scores, axis=0) + jnp.concatenate(biases, axis=0)
        m = jnp.max(s, axis=-1, keepdims=True)
        p = jnp.exp(s - m)
        l = jnp.sum(p, axis=-1, keepdims=True)
        pn = (p * (1.0 / l)).astype(BF16)
        for g, (di, _, _, vv, out_rows) in enumerate(units):
            vb = vv.astype(BF16)
            r0 = 2 * g * KEYS_BACK
            r1 = r0 + KEYS_BACK
            o0 = jnp.dot(pn[r0:r1], vb, preferred_element_type=F32)
            o1 = jnp.dot(pn[r1:r1 + KEYS_BACK], vb, preferred_element_type=F32)
            od_ref[di, out_rows, :] = jnp.where(lo, o0, o1)
            m_ref[di, out_rows, :] = jnp.where(lo, m[r0:r1], m[r1:r1 + KEYS_BACK])
            l_ref[di, out_rows, :] = jnp.where(lo, l[r0:r1], l[r1:r1 + KEYS_BACK])

    def loop(n, group, make_unit):
        assert n % group == 0

        def step(t, carry):
            attend([make_unit(t * group + g) for g in range(group)])
            return carry
        lax.fori_loop(0, n // group, step, 0)

    blocks = seq // KEYS_BACK
    sub_blocks = per // KEYS_BACK

    def dense_unit(start):
        rows = pl.ds(start, KEYS_BACK)
        keys = rows if isinstance(start, int) and start == 0 else pl.ds(start - KEYS_BACK, 2 * KEYS_BACK)
        return 0, q_ref[rows, :] * B_SCALE, k_ref[keys, :], v_ref[keys, :], rows

    def mid_unit(start, first):
        rows = pl.ds(start, KEYS_BACK)
        keys = rows if first else pl.ds(start - KEYS_BACK, 2 * KEYS_BACK)
        return 1, qd_ref[rows, :], kd_ref[keys, :], vd_ref[keys, :], rows

    def wide_unit(r):
        rows = pl.ds((r & (RES - 1)) * per + (r >> _log2(RES)), KEYS_BACK, stride=RES)
        return 2, qd_ref[rows, :], kd_ref[rows, :], vd_ref[rows, :], rows

    attend([dense_unit(0)] + [mid_unit(c * per, True) for c in range(RES)])
    def group_of(n):
        return max(f for f in range(1, UNITS_IN_FLIGHT + 1) if n % f == 0)

    loop(blocks - 1, group_of(blocks - 1), lambda u: dense_unit(pl.multiple_of((u + 1) * KEYS_BACK, KEYS_BACK)))
    loop(RES * (sub_blocks - 1), group_of(RES * (sub_blocks - 1)),
         lambda u: mid_unit(pl.multiple_of((u & (RES - 1)) * per + (1 + (u >> _log2(RES))) * KEYS_BACK, KEYS_BACK),
                            False))
    loop(RES * RES, group_of(RES * RES), wide_unit)

    for c in range(RES):
        for j in range(sub_blocks):
            nat = pl.ds(c + RES * j * KEYS_BACK, KEYS_BACK, stride=RES)
            grp = pl.ds(c * per + j * KEYS_BACK, KEYS_BACK)
            ms = [m_ref[0, nat, :], m_ref[1, grp, :], m_ref[2, grp, :]]
            ls = [l_ref[0, nat, :], l_ref[1, grp, :], l_ref[2, grp, :]]
            outs = [od_ref[0, nat, :], od_ref[1, grp, :], od_ref[2, grp, :]]
            top = jnp.maximum(jnp.maximum(ms[0], ms[1]), ms[2])
            ws = [l * jnp.exp(m - top) for m, l in zip(ms, ls)]
            inv = 1.0 / (ws[0] + ws[1] + ws[2])
            o_ref[nat, :] = (ws[0] * inv) * outs[0] + (ws[1] * inv) * outs[1] + (ws[2] * inv) * outs[2]


def _dilated_attention_prompt(q, k, v, bias_tabs, batch, seq):
    assert DILATIONS == ((KEYS_BACK, 1), (KEYS_BACK * RES, RES), (KEYS_BACK * RES * RES, RES * RES))
    assert seq == KEYS_BACK * RES * RES
    pairs = B_WIDTH // LANES
    blk = pl.BlockSpec((seq, LANES), lambda b, p: (b, p))
    blk_t = pl.BlockSpec((LANES, seq), lambda b, p: (b * pairs + p, 0))
    out_t = jax.ShapeDtypeStruct((batch * B_WIDTH, seq), F32)
    return _call(functools.partial(_dilated_prompt_body, seq=seq), (batch, pairs),
                 [blk, blk, blk, pl.BlockSpec((N_DIL, 2, KEYS_BACK, 2 * KEYS_BACK), lambda b, p: (0, p, 0, 0))],
                 [blk, blk_t, blk_t], [jax.ShapeDtypeStruct((batch * seq, B_WIDTH), F32), out_t, out_t],
                 (q, k, v, bias_tabs), "dilated_attention_prompt",
                 scratch=[pltpu.VMEM((seq, LANES), F32)] * 3 + [pltpu.VMEM((N_DIL, seq, LANES), F32)] * 3)


def _sample_attn_body(q_ref, kn_ref, vn_ref, kt_ref, vt_ref, tf_ref, tc_ref, tn_ref, qm_ref, mk_ref, mv_ref,
                      yb_ref, ym_ref, *, ds, far):
    rows = B_HEADS * ds
    row = lax.broadcasted_iota(jnp.int32, (rows, B_WIDTH), 0)
    col = lax.broadcasted_iota(jnp.int32, (rows, B_WIDTH), 1)
    own = (row >> _log2(ds)) == (col >> _log2(B_HEAD_DIM))
    q = q_ref[...] * B_SCALE
    qbd = jnp.where(own, jnp.concatenate([q] * B_HEADS, axis=0), 0.0).astype(BF16)
    s_f = _mm(qbd, kt_ref[:, :far]) + tf_ref[...]
    s_c = _mm(qbd, kt_ref[:, far:])
    s_n = _mm(qbd, kn_ref[...], _NT)
    pc, pn, lses = [], [], []
    p_far = None
    for di in range(N_DIL):
        widest = di == N_DIL - 1
        sc = s_c + tc_ref[di]
        sn = s_n + tn_ref[di]
        m = jnp.maximum(jnp.max(sc, axis=-1, keepdims=True), jnp.max(sn, axis=-1, keepdims=True))
        if widest:
            m = jnp.maximum(m, jnp.max(s_f, axis=-1, keepdims=True))
        ec = jnp.exp(sc - m)
        en = jnp.exp(sn - m)
        l = jnp.sum(ec, axis=-1, keepdims=True) + jnp.sum(en, axis=-1, keepdims=True)
        if widest:
            ef = jnp.exp(s_f - m)
            l = l + jnp.sum(ef, axis=-1, keepdims=True)
        inv = 1.0 / l
        if widest:
            p_far = (ef * inv).astype(BF16)
        pc.append((ec * inv).astype(BF16))
        pn.append((en * inv).astype(BF16))
        lses.append(m + jnp.log(l))
    o = (_mm(jnp.concatenate(pc, axis=0), vt_ref[:, far:], _NT)
         + _mm(jnp.concatenate(pn, axis=0), vn_ref[...]))
    o_far = _mm(p_far, vt_ref[:, :far], _NT)
    top = jnp.maximum(jnp.maximum(lses[0], lses[1]), lses[2])
    ws = [jnp.exp(x - top) for x in lses]
    den = ws[0] + ws[1] + ws[2]
    mixed = (ws[N_DIL - 1] / den) * (o[(N_DIL - 1) * rows:] + o_far)
    for di in range(N_DIL - 1):
        mixed = mixed + (ws[di] / den) * o[di * rows:(di + 1) * rows]
    mixed = jnp.where(own, mixed, 0.0)
    yb = mixed[0:ds]
    for h in range(1, B_HEADS):
        yb = yb + mixed[h * ds:(h + 1) * ds]
    yb_ref[...] = yb
    _memory_attend_heads(qm_ref[...], mk_ref, mv_ref, ym_ref)


def _attention_sample(q, k_new, v_new, cache_k, cache_v, tabs, qm, cache_mk, cache_mv, ds):
    batch, n_cache = cache_k.shape[:2]
    rows = B_HEADS * ds
    far = n_cache - DILATIONS[-2][0]
    assert N_DIL == 3 and rows == LANES and far % LANES == 0
    assert all(w0 <= w1 for (w0, _), (w1, _) in zip(DILATIONS, DILATIONS[1:]))

    def keys_minor(x):
        return jnp.transpose(x, (0, 2, 3, 1)).reshape(batch * B_WIDTH, n_cache)

    new = pl.BlockSpec((ds, B_WIDTH), lambda b: (b, 0))
    cache = pl.BlockSpec((B_WIDTH, n_cache), lambda b: (b, 0))
    mem = pl.BlockSpec((N_MEM, M_HEADS, M_HEAD_DIM), lambda b: (b, 0, 0))
    cache_mk = cache_mk.reshape(batch * N_MEM, M_HEADS, M_HEAD_DIM)
    cache_mv = cache_mv.reshape(batch * N_MEM, M_HEADS, M_HEAD_DIM)
    qm_spec = pl.BlockSpec((ds, M_WIDTH), lambda b: (b, 0))
    return _call(functools.partial(_sample_attn_body, ds=ds, far=far), (batch,),
                 [new, new, new, cache, cache, _const_spec((rows, far)), _const_spec((N_DIL, rows, n_cache - far)),
                  _const_spec((N_DIL, rows, ds)), qm_spec, mem, mem],
                 [new, qm_spec],
                 [jax.ShapeDtypeStruct((batch * ds, B_WIDTH), F32), jax.ShapeDtypeStruct((batch * ds, M_WIDTH), F32)],
                 (q, k_new, v_new, keys_minor(cache_k), keys_minor(cache_v), tabs[N_DIL - 1, :, :far],
                  tabs[:, :, far:n_cache], tabs[:, :, n_cache:], qm, cache_mk, cache_mv),
                 "attention_sample")


def _out_router_body(ya_ref, yb_ref, ym_ref, x_ref, wo_ref, gf_ref, wr_ref, rec_ref):
    o_b = A_WIDTH
    o_m = A_WIDTH + B_WIDTH
    tm, d = x_ref.shape
    sub = min(tm, ROUTER_SUB_ROWS)
    lane_i = lax.broadcasted_iota(jnp.int32, (sub, LANES), 1)
    lane = lane_i.astype(F32)
    big = float(LANES)

    def first_argmax(vals):
        top = jnp.max(vals, axis=-1, keepdims=True)
        return top, jnp.min(jnp.where(vals == top, lane, big), axis=-1, keepdims=True)

    normed = []
    for r0 in range(0, tm, sub):
        rows = slice(r0, r0 + sub)
        mix = (_mm(ya_ref[rows, :], wo_ref[0:o_b, :]) + _mm(yb_ref[rows, :], wo_ref[o_b:o_m, :])
               + _mm(ym_ref[rows, :], wo_ref[o_m:, :]))
        h = x_ref[rows, :] + mix
        rec_ref[rows, :d] = h
        normed.append(_row_norm(h, gf_ref))
    for r0, n2 in zip(range(0, tm, sub), normed):
        logits = _mm(n2, wr_ref[...])
        is_group = (lane_i >= N_EXPERTS) & (lane_i < N_EXPERTS + N_GROUPS)
        _, g_lane = first_argmax(jnp.where(is_group, logits, NEG_INF))
        grp = g_lane.astype(jnp.int32) - N_EXPERTS
        in_grp = (lane_i < N_EXPERTS) & ((lane_i >> _log2(EXPERTS_PER_GROUP)) == grp)
        e_log = jnp.where(in_grp, logits, NEG_INF)
        v1, i1 = first_argmax(e_log)
        v2, i2 = first_argmax(jnp.where(lane == i1, NEG_INF, e_log))
        e2 = jnp.exp(v2 - v1)
        den = 1.0 + e2
        rec_ref[r0:r0 + sub, d:] = (jnp.where(lane == i1, 1.0 / den, 0.0) + jnp.where(lane == i2, e2 / den, 0.0)
                                    + jnp.where(lane_i == N_EXPERTS, grp.astype(F32), 0.0))


def _out_and_router(ya, yb, ym, x, w_out, g_ffn, w_router, tm, name):
    rows, d = x.shape

    def rs(width):
        return pl.BlockSpec((tm, width), lambda i: (i, 0))

    return _call(_out_router_body, (rows // tm,),
                 [rs(A_WIDTH), rs(B_WIDTH), rs(M_WIDTH), rs(d), _const_spec((d, d)), _const_spec((1, d)),
                  _const_spec((d, LANES))],
                 rs(d + LANES), jax.ShapeDtypeStruct((rows, d + LANES), F32),
                 (ya, yb, ym, x, w_out, g_ffn.reshape(1, d), w_router), name)


def _expert_ffn(n2, gates, expert_lane, wg_ref, wu_ref, wd_ref):
    hg = jnp.dot(n2, wg_ref[0], preferred_element_type=F32)
    hu = jnp.dot(n2, wu_ref[0], preferred_element_type=F32)
    lane = lax.broadcasted_iota(jnp.int32, gates.shape, 1)
    g = jnp.sum(jnp.where(lane == expert_lane, gates, 0.0), axis=-1, keepdims=True)
    act = hg * jax.nn.sigmoid(hg) * hu * g
    return jnp.dot(act.astype(BF16), wd_ref[0], preferred_element_type=F32)


def _moe_dense_body(rec_ref, gf_ref, wg_ref, wu_ref, wd_ref, o_ref, n2_ref):
    e = pl.program_id(1)
    d = o_ref.shape[1]

    @pl.when(e == 0)
    def _():
        h = rec_ref[:, :d]
        o_ref[...] = h
        n2_ref[...] = _row_norm(h, gf_ref)

    o_ref[...] += _expert_ffn(n2_ref[...], rec_ref[:, d:], e, wg_ref, wu_ref, wd_ref)


def _moe_dense(rec, g_ffn, w_gate, w_up, w_down, tm, name):
    rows = rec.shape[0]
    d, ff = w_gate.shape[1:]
    wspec = lambda a, b: pl.BlockSpec((1, a, b), lambda i, e: (e, 0, 0))
    return _call(_moe_dense_body, (rows // tm, N_EXPERTS),
                 [pl.BlockSpec((tm, d + LANES), lambda i, e: (i, 0)), _const_spec((1, d)),
                  wspec(d, ff), wspec(d, ff), wspec(ff, d)],
                 pl.BlockSpec((tm, d), lambda i, e: (i, 0)), jax.ShapeDtypeStruct((rows, d), F32),
                 (rec, g_ffn.reshape(1, d), w_gate, w_up, w_down), name,
                 semantics=("parallel", "arbitrary"), scratch=[pltpu.VMEM((tm, d), BF16)])


def _group_sort_plan(rec, tm):
    n, width = rec.shape
    d = width - LANES
    n_tiles = n // tm + N_GROUPS
    grp = rec[:, d + N_EXPERTS].astype(jnp.int32)
    order = jnp.argsort(grp, stable=True).astype(jnp.int32)
    counts = jnp.sum((grp[:, None] == jnp.arange(N_GROUPS)[None, :]).astype(jnp.int32), axis=0)
    tiles_per = (counts + tm - 1) // tm
    tile_end = jnp.cumsum(tiles_per)
    tile_start = tile_end - tiles_per
    tok_start = jnp.cumsum(counts) - counts
    n_used = tile_end[-1]
    t = jnp.arange(n_tiles, dtype=jnp.int32)
    tgrp = jnp.minimum(jnp.sum((t[:, None] >= tile_end[None, :]).astype(jnp.int32), axis=1), N_GROUPS - 1)
    row0 = (t - tile_start[tgrp]) * tm
    nvalid = jnp.where(t < n_used, jnp.clip(counts[tgrp] - row0, 0, tm), 0).astype(jnp.int32)
    k = row0[:, None] + jnp.arange(tm, dtype=jnp.int32)[None, :]
    real = jnp.arange(tm, dtype=jnp.int32)[None, :] < nvalid[:, None]
    src = jnp.where(real, order[jnp.clip(tok_start[tgrp][:, None] + k, 0, n - 1)], -1)
    return src.reshape(-1).astype(jnp.int32), tgrp.astype(jnp.int32), nvalid, n_used.reshape(1).astype(jnp.int32)


def _moe_grouped_body(src_ref, tgrp_ref, nvalid_ref, nused_ref, rec_hbm, gf_ref, wg_ref, wu_ref, wd_ref, y_hbm,
                      rec_buf, n2_ref, acc_ref, y_buf, gsem, ssem, *, tm):
    i = pl.program_id(0)
    e = pl.program_id(1)
    d = y_buf.shape[1]
    n_used = nused_ref[0]
    slot = i % 2

    def gather_rows(tile, slot_):
        def block(k, c):
            r0 = pl.multiple_of(k * DMA_ROWS_PER_TRIP, DMA_ROWS_PER_TRIP)
            for j in range(DMA_ROWS_PER_TRIP):
                tok = jnp.maximum(src_ref[tile * tm + r0 + j], 0)
                pltpu.make_async_copy(rec_hbm.at[pl.ds(tok, 1)], rec_buf.at[slot_, pl.ds(r0 + j, 1)],
                                      gsem.at[slot_]).start()
            return c
        lax.fori_loop(0, tm // DMA_ROWS_PER_TRIP, block, 0)

    def wait_gather(slot_):
        pltpu.make_async_copy(rec_hbm.at[pl.ds(0, tm)], rec_buf.at[slot_], gsem.at[slot_]).wait()

    def scatter_copy(tile, r):
        tok = src_ref[tile * tm + r]
        return pltpu.make_async_copy(y_buf.at[pl.ds(r, 1)], y_hbm.at[pl.ds(tok, 1)], ssem.at[0])

    def finish_rows(rows):
        y_buf[rows, :] = rec_buf[slot, rows, :d] + acc_ref[rows, :]

    def scatter_rows(tile, wait):
        def row(r, c):
            cp = scatter_copy(tile, r)
            if wait:
                cp.wait()
            else:
                cp.start()
            return c

        def finish_and_start_block(k, c):
            r0 = pl.multiple_of(k * DMA_ROWS_PER_TRIP, DMA_ROWS_PER_TRIP)
            finish_rows(pl.ds(r0, DMA_ROWS_PER_TRIP))
            for j in range(DMA_ROWS_PER_TRIP):
                scatter_copy(tile, r0 + j).start()
            return c

        full = nvalid_ref[tile] == tm

        @pl.when(full)
        def _():
            if wait:
                pltpu.make_async_copy(y_buf, y_hbm.at[pl.ds(0, tm)], ssem.at[0]).wait()
            else:
                lax.fori_loop(0, tm // DMA_ROWS_PER_TRIP, finish_and_start_block, 0)

        @pl.when(jnp.logical_not(full))
        def _():
            if not wait:
                finish_rows(pl.ds(0, tm))
            lax.fori_loop(0, nvalid_ref[tile], row, 0)

    @pl.when(i < n_used)
    def _():
        @pl.when(e == 0)
        def _():
            @pl.when(i == 0)
            def _():
                gather_rows(0, 0)

            wait_gather(slot)

            @pl.when(i + 1 < n_used)
            def _():
                gather_rows(i + 1, 1 - slot)

            n2_ref[...] = _row_norm(rec_buf[slot, :, :d], gf_ref)
            acc_ref[...] = jnp.zeros_like(acc_ref)

        lane0 = tgrp_ref[i] * EXPERTS_PER_GROUP
        acc_ref[...] += _expert_ffn(n2_ref[...], rec_buf[slot, :, d:], lane0 + e, wg_ref, wu_ref, wd_ref)

        @pl.when(e == EXPERTS_PER_GROUP - 1)
        def _():
            @pl.when(i > 0)
            def _():
                scatter_rows(i - 1, True)

            scatter_rows(i, False)

            @pl.when(i == n_used - 1)
            def _():
                scatter_rows(i, True)


def _moe_grouped(rec, g_ffn, w_gate, w_up, w_down, tm, name):
    n = rec.shape[0]
    d, ff = w_gate.shape[1:]
    src, tgrp, nvalid, n_used = _group_sort_plan(rec, tm)
    n_tiles = tgrp.shape[0]

    def widx(i, e, src_ref, tgrp_ref, nvalid_ref, nused_ref):
        last = nused_ref[0] - 1
        live = i <= last
        return (tgrp_ref[jnp.minimum(i, last)] * EXPERTS_PER_GROUP
                + jnp.where(live, e, EXPERTS_PER_GROUP - 1), 0, 0)

    scratch = [pltpu.VMEM((2, tm, d + LANES), F32), pltpu.VMEM((tm, d), BF16), pltpu.VMEM((tm, d), F32),
               pltpu.VMEM((tm, d), F32), pltpu.SemaphoreType.DMA((2,)), pltpu.SemaphoreType.DMA((1,))]
    blk = 3 * _nbytes((d, ff), BF16) + _nbytes((1, d), F32)
    scr = sum(_nbytes(s.shape, s.dtype) for s in scratch[:4])
    return pl.pallas_call(
        functools.partial(_moe_grouped_body, tm=tm),
        grid_spec=pltpu.PrefetchScalarGridSpec(
            num_scalar_prefetch=4,
            grid=(n_tiles, EXPERTS_PER_GROUP),
            in_specs=[pl.BlockSpec(memory_space=pl.ANY),
                      pl.BlockSpec((1, d), lambda i, e, *_: (0, 0)),
                      pl.BlockSpec((1, d, ff), widx), pl.BlockSpec((1, d, ff), widx), pl.BlockSpec((1, ff, d), widx)],
            out_specs=pl.BlockSpec(memory_space=pl.ANY),
            scratch_shapes=scratch),
        out_shape=jax.ShapeDtypeStruct((n, d), F32),
        compiler_params=pltpu.CompilerParams(dimension_semantics=("arbitrary", "arbitrary"),
                                             vmem_limit_bytes=_vmem_limit(blk, scr)),
        name=name,
    )(src, tgrp, nvalid, n_used, rec, g_ffn.reshape(1, d), w_gate, w_up, w_down)


def kernel(x_prompt, x_sample, mem_prompt, cache_win_k, cache_win_v, cache_mem_k, cache_mem_v, rel_bias, g_mix, w_in, g_av, w_s, b_s, g_qb, g_kb, g_qm, g_km, g_mem, w_mem_kv, w_out, g_ffn, w_router_group, w_router_expert, w_gate, w_up, w_down):
    batch, seq, d = x_prompt.shape
    dec_batch, ds, _ = x_sample.shape
    depth = w_in.shape[0]
    n_cache = cache_win_k.shape[2]
    assert seq % (KEYS_BACK * DILATIONS[-1][1]) == 0 and seq <= DILATIONS[-1][0] and seq % CHUNK == 0
    assert n_cache >= DILATIONS[-1][0] and ds <= CHUNK

    xp = x_prompt.reshape(batch * seq, d)
    xs = x_sample.reshape(dec_batch * ds, d)
    mem = mem_prompt.reshape(batch * N_MEM, d)
    rows_s = dec_batch * ds
    tm = _row_tile(batch * seq)

    bias_prompt = _prompt_bias_tables(rel_bias)
    bias_sample = _sample_bias_tables(rel_bias, n_cache, ds)
    causal = np.tril(np.ones((CHUNK, CHUNK), bool))
    same_row = (np.arange(rows_s)[:, None] // ds) == (np.arange(rows_s)[None, :] // ds)

    outs = [[] for _ in range(8)]
    for l in range(depth):
        w_in_b = w_in[l].astype(BF16)
        w_out_b = w_out[l].astype(BF16)
        w_router = jnp.concatenate(
            [jnp.transpose(w_router_expert[l], (1, 0, 2)).reshape(d, N_EXPERTS), w_router_group[l],
             jnp.zeros((d, LANES - N_EXPERTS - N_GROUPS), F32)], axis=1).astype(BF16)
        wg_b, wu_b, wd_b = w_gate[l].astype(BF16), w_up[l].astype(BF16), w_down[l].astype(BF16)
        ws_prompt = jnp.where(causal, w_s[l], 0.0)
        bs_prompt = b_s[l].T
        ws_small = jnp.where(causal[:ds, :ds], w_s[l][:, :ds, :ds], 0.0)
        ws_sample = jnp.where(same_row, jnp.tile(ws_small, (1, dec_batch, dec_batch)), 0.0)
        bs_sample = jnp.tile(b_s[l][:, :ds].T, (dec_batch, 1))

        mk, mv = _memory_kv(mem, g_mem[l], w_mem_kv[l].astype(BF16), g_km[l])
        ya, va, qb, kb, vb, ym = _mixer_inputs(xp, g_mix[l], w_in_b, g_av[l], ws_prompt, bs_prompt, g_qb[l],
                                               g_kb[l], g_qm[l], (mk, mv, seq), tm, "mixer_inputs_prompt")
        yb, kb_t, vb_t = _dilated_attention_prompt(qb, kb, vb, bias_prompt, batch, seq)
        rec = _out_and_router(ya, yb, ym, xp, w_out_b, g_ffn[l], w_router, tm, "out_router_prompt")
        xp = _moe_grouped(rec, g_ffn[l], wg_b, wu_b, wd_b, tm, "moe_prompt")
        wp = min(DILATIONS[-1][0], seq)
        for out, x_t in ((outs[0], kb_t), (outs[1], vb_t)):
            x = jnp.transpose(x_t.reshape(batch, B_HEADS, B_HEAD_DIM, seq), (0, 3, 1, 2))
            out.append(x[:, seq - wp:])
        outs[2].append(va.reshape(batch, seq, A_WIDTH)[:, seq - CHUNK:].reshape(batch, CHUNK, A_GROUPS, A_DIM))
        outs[3].append(mk.reshape(batch, N_MEM, M_HEADS, M_HEAD_DIM))
        outs[4].append(mv.reshape(batch, N_MEM, M_HEADS, M_HEAD_DIM))

        ya, va, qb, kb, vb, qm = _mixer_inputs(xs, g_mix[l], w_in_b, g_av[l], ws_sample, bs_sample, g_qb[l],
                                               g_kb[l], g_qm[l], None, rows_s, "mixer_inputs_sample")
        yb, ym = _attention_sample(
            qb, kb, vb, cache_win_k[l], cache_win_v[l], bias_sample, qm, cache_mem_k[l], cache_mem_v[l], ds)
        rec = _out_and_router(ya, yb, ym, xs, w_out_b, g_ffn[l], w_router, rows_s, "out_router_sample")
        xs = _moe_dense(rec, g_ffn[l], wg_b, wu_b, wd_b, rows_s, "moe_sample")
        outs[5].append(kb.reshape(dec_batch, ds, B_HEADS, B_HEAD_DIM))
        outs[6].append(vb.reshape(dec_batch, ds, B_HEADS, B_HEAD_DIM))
        outs[7].append(va.reshape(dec_batch, ds, A_GROUPS, A_DIM))

    stacked = [jnp.stack(o, axis=0) for o in outs]
    return (xp.reshape(batch, seq, d), xs.reshape(dec_batch, ds, d), *stacked)
```
